```python
import jax, jax.numpy as jnp
from jax import lax
import numpy as np

D_MODEL = 2048
BATCH = 4
SEQ = 4096
DEPTH = 1

MIX_WIDTH = D_MODEL
GLA_HEADS = 4
GLA_DV = MIX_WIDTH // 2 // GLA_HEADS
GLA_DK = GLA_DV // 2
GLA_RANK = 16
GLA_TAU = 16.0
GLA_CHUNK = 64
ATT_HEADS = 16
ATT_HD = (MIX_WIDTH - GLA_HEADS * GLA_DV) // ATT_HEADS
DILATED_PATTERNS = ((128, 1), (512, 4), (2048, 16))
PEER_NKEYS = 128
PEER_EXPERTS = PEER_NKEYS * PEER_NKEYS
PEER_HEADS = 8
PEER_TOPK = 16
PEER_DQ = 256
PEER_TOKEN_BLOCK = 64
EPS = 1e-6
NEG = -1e30

GLA_QK = GLA_HEADS * GLA_DK
GLA_V = GLA_HEADS * GLA_DV
ATT_W = ATT_HEADS * ATT_HD
IN_SPLITS = (GLA_QK, GLA_QK, GLA_V, GLA_V, GLA_RANK, GLA_RANK, ATT_W, ATT_W, ATT_W)
IN_WIDTH = sum(IN_SPLITS)

kernel_name = "hymba_gla_dilated_peer_encoder"


def rms_norm(x, g):
    xf = x.astype(jnp.float32)
    y = xf * lax.rsqrt(jnp.mean(xf * xf, axis=-1, keepdims=True) + EPS)
    return (y * g.astype(jnp.float32)).astype(x.dtype)


def alibi_slopes(n):
    return (2.0 ** (-8.0 * np.arange(1, n + 1) / n)).astype(np.float32)


def gla_chunked(q, k, v, log_a):
    B, H, S, dk = q.shape
    dv = v.shape[-1]
    C = GLA_CHUNK
    N = S // C
    q = q.reshape(B, H, N, C, dk)
    k = k.reshape(B, H, N, C, dk)
    v = v.reshape(B, H, N, C, dv)
    b = jnp.cumsum(log_a.reshape(B, H, N, C, dk), axis=3)
    b_last = b[:, :, :, -1:, :]
    q_d = q * jnp.exp(b)
    k_d = k * jnp.exp(-b)
    causal = jnp.tril(jnp.ones((C, C), dtype=bool))
    att = jnp.where(causal, jnp.einsum('bhncd,bhnsd->bhncs', q_d, k_d), 0.0)
    o_intra = jnp.einsum('bhncs,bhnse->bhnce', att, v)
    kv = jnp.einsum('bhncd,bhnce->bhnde', k * jnp.exp(b_last - b), v)
    decay = jnp.exp(b_last[:, :, :, 0, :])

    def step(state, inp):
        dec, kv_n = inp
        return dec[..., None] * state + kv_n, state

    init = jnp.zeros((B, H, dk, dv), q.dtype)
    _, prev = lax.scan(step, init, (jnp.moveaxis(decay, 2, 0), jnp.moveaxis(kv, 2, 0)))
    prev = jnp.moveaxis(prev, 0, 2)
    o_inter = jnp.einsum('bhncd,bhnde->bhnce', q_d, prev)
    return (o_intra + o_inter).reshape(B, H, S, dv)


def dilated_band(q, k, v, slopes, dilation, radius):
    B, H, S, hd = q.shape
    L = S // dilation
    blk = radius
    nb = -(-L // blk)
    Lp = nb * blk

    def to_sub(t):
        return t.reshape(B, H, L, dilation, hd).transpose(0, 1, 3, 2, 4)

    qs = jnp.pad(to_sub(q), ((0, 0),) * 3 + ((0, Lp - L), (0, 0))).reshape(B, H, dilation, nb, blk, hd)
    pad_kv = ((0, 0),) * 3 + ((blk, Lp - L + blk), (0, 0))

    def windows(t):
        t = jnp.pad(to_sub(t), pad_kv).reshape(B, H, dilation, nb + 2, blk, hd)
        return jnp.concatenate([t[:, :, :, :-2], t[:, :, :, 1:-1], t[:, :, :, 2:]], axis=4)

    kw, vw = windows(k), windows(v)
    q_pos = jnp.arange(nb)[:, None] * blk + jnp.arange(blk)[None, :]
    k_pos = jnp.arange(nb)[:, None] * blk - blk + jnp.arange(3 * blk)[None, :]
    off = k_pos[:, None, :] - q_pos[:, :, None]
    valid = (jnp.abs(off) <= radius) & (k_pos[:, None, :] >= 0) & (k_pos[:, None, :] < L)
    dist = (jnp.abs(off) * dilation).astype(jnp.float32)
    s = jnp.einsum('bhrnqd,bhrnkd->bhrnqk', qs, kw)
    s = s - slopes[None, :, None, None, None, None] * dist
    s = jnp.where(valid, s, NEG)
    m = jnp.max(s, axis=-1)
    p = jnp.exp(s - m[..., None])
    l = jnp.sum(p, axis=-1)
    num = jnp.einsum('bhrnqk,bhrnkd->bhrnqd', p, vw)

    def from_sub(t):
        t = t.reshape((B, H, dilation, Lp) + t.shape[5:])[:, :, :, :L]
        t = jnp.moveaxis(t, 2, 3)
        return t.reshape((B, H, S) + t.shape[4:])

    return from_sub(num), from_sub(m), from_sub(l)


def hybrid_mixer(h, w_in, gate_up_f, gate_bias_f, gate_up_b, gate_bias_b, gla_norm_g, att_norm_g, w_out):
    B, S, _ = h.shape
    f32 = jnp.float32
    proj = jnp.einsum('bsd,de->bse', h, w_in)
    offsets = [int(o) for o in np.cumsum(IN_SPLITS)[:-1]]
    gq, gk, gv, gr, gdf, gdb, aq, ak, av = jnp.split(proj, offsets, axis=-1)

    def heads(t, n):
        return t.reshape(B, S, n, -1).transpose(0, 2, 1, 3).astype(f32)

    q = heads(gq, GLA_HEADS) * (GLA_DK ** -0.5)
    k = heads(gk, GLA_HEADS)
    v = heads(gv, GLA_HEADS)

    def log_gate(down, up, bias):
        z = (jnp.einsum('bsr,rk->bsk', down, up) + bias).astype(f32)
        return heads(jax.nn.log_sigmoid(z) / GLA_TAU, GLA_HEADS)

    la_f = log_gate(gdf, gate_up_f, gate_bias_f)
    la_b = log_gate(gdb, gate_up_b, gate_bias_b)
    flip = lambda t: jnp.flip(t, axis=2)
    o_f = gla_chunked(q, k, v, la_f)
    o_b = flip(gla_chunked(flip(q), flip(k), flip(v), flip(la_b)))
    o = rms_norm(o_f + o_b, gla_norm_g)
    o = o.transpose(0, 2, 1, 3).reshape(B, S, GLA_V)
    gla_out = o * jax.nn.silu(gr.astype(f32))

    q = heads(aq, ATT_HEADS) * (ATT_HD ** -0.5)
    k = heads(ak, ATT_HEADS)
    v = heads(av, ATT_HEADS)
    slopes = jnp.asarray(alibi_slopes(ATT_HEADS))
    nums, maxs, dens = [], [], []
    for window, dilation in DILATED_PATTERNS:
        n_, m_, l_ = dilated_band(q, k, v, slopes, dilation, window // (2 * dilation))
        nums.append(n_); maxs.append(m_); dens.append(l_)
    nums, maxs, dens = jnp.stack(nums), jnp.stack(maxs), jnp.stack(dens)
    w = jnp.exp(maxs - jnp.max(maxs, axis=0, keepdims=True))
    att = jnp.sum(w[..., None] * nums, axis=0) / jnp.sum(w * dens, axis=0)[..., None]
    att = att.transpose(0, 2, 1, 3).reshape(B, S, ATT_W)
    att = rms_norm(att, att_norm_g)

    y = jnp.concatenate([gla_out, att], axis=-1).astype(h.dtype)
    return jnp.einsum('bse,ed->bsd', y, w_out)


def peer_ffn(h, w_q, sub_keys, expert_u, expert_v):
    B, S, D = h.shape
    T = B * S
    f32 = jnp.float32
    x = h.reshape(T, D)
    q = jnp.einsum('td,de->te', x, w_q).reshape(T, PEER_HEADS, 2, PEER_DQ // 2).astype(f32)
    s = jnp.einsum('thcd,hckd->thck', q, sub_keys.astype(f32))
    sc, si = lax.top_k(s, PEER_TOPK)
    cand = (sc[:, :, 0, :, None] + sc[:, :, 1, None, :]).reshape(T, PEER_HEADS, PEER_TOPK * PEER_TOPK)
    top_sc, top_ci = lax.top_k(cand, PEER_TOPK)
    i1 = jnp.take_along_axis(si[:, :, 0], top_ci // PEER_TOPK, axis=-1)
    i2 = jnp.take_along_axis(si[:, :, 1], top_ci % PEER_TOPK, axis=-1)
    experts = i1 * PEER_NKEYS + i2
    gates = jax.nn.softmax(top_sc, axis=-1)
    TB = PEER_TOKEN_BLOCK
    nblk = T // TB

    def block(args):
        xb, eb, gb = args
        a = jax.nn.gelu(jnp.einsum('td,thkd->thk', xb, expert_u[eb]).astype(f32), approximate=False)
        wgt = (gb * a).astype(xb.dtype)
        return jnp.einsum('thk,thkd->td', wgt, expert_v[eb])

    out = lax.map(block, (x.reshape(nblk, TB, D),
                          experts.reshape(nblk, TB, PEER_HEADS, PEER_TOPK),
                          gates.reshape(nblk, TB, PEER_HEADS, PEER_TOPK)))
    return out.reshape(B, S, D).astype(h.dtype)


def setup_inputs(seed: int = 0) -> dict:
    key = jax.random.key(seed)
    ks = jax.random.split(key, 20)
    nrm = lambda k, shape, scale: jax.random.normal(k, shape, jnp.float32) * scale
    L = DEPTH
    return {
        "x": nrm(ks[0], (BATCH, SEQ, D_MODEL), 1.0),
        "norm1_g": 1.0 + nrm(ks[1], (L, D_MODEL), 0.02),
        "w_in": nrm(ks[2], (L, D_MODEL, IN_WIDTH), D_MODEL ** -0.5),
        "gla_gate_up_f": nrm(ks[3], (L, GLA_RANK, GLA_QK), GLA_RANK ** -0.5),
        "gla_gate_bias_f": nrm(ks[4], (L, GLA_QK), 0.1),
        "gla_gate_up_b": nrm(ks[5], (L, GLA_RANK, GLA_QK), GLA_RANK ** -0.5),
        "gla_gate_bias_b": nrm(ks[6], (L, GLA_QK), 0.1),
        "gla_norm_g": 1.0 + nrm(ks[7], (L, GLA_DV), 0.02),
        "att_norm_g": 1.0 + nrm(ks[8], (L, ATT_W), 0.02),
        "w_out": nrm(ks[9], (L, MIX_WIDTH, D_MODEL), MIX_WIDTH ** -0.5),
        "norm2_g": 1.0 + nrm(ks[10], (L, D_MODEL), 0.02),
        "peer_w_q": nrm(ks[11], (L, D_MODEL, PEER_HEADS * PEER_DQ), D_MODEL ** -0.5),
        "peer_sub_keys": nrm(ks[12], (L, PEER_HEADS, 2, PEER_NKEYS, PEER_DQ // 2), (PEER_DQ // 2) ** -0.5),
        "peer_u": nrm(ks[13], (L, PEER_EXPERTS, D_MODEL), D_MODEL ** -0.5),
        "peer_v": nrm(ks[14], (L, PEER_EXPERTS, D_MODEL), 0.5),
        "final_norm_g": 1.0 + nrm(ks[15], (D_MODEL,), 0.02),
    }


def reference(x, norm1_g, w_in, gla_gate_up_f, gla_gate_bias_f, gla_gate_up_b, gla_gate_bias_b,
              gla_norm_g, att_norm_g, w_out, norm2_g, peer_w_q, peer_sub_keys, peer_u, peer_v,
              final_norm_g):
    for l in range(DEPTH):
        h = x + hybrid_mixer(rms_norm(x, norm1_g[l]), w_in[l], gla_gate_up_f[l], gla_gate_bias_f[l],
                             gla_gate_up_b[l], gla_gate_bias_b[l], gla_norm_g[l], att_norm_g[l], w_out[l])
        x = h + peer_ffn(rms_norm(h, norm2_g[l]), peer_w_q[l], peer_sub_keys[l], peer_u[l], peer_v[l])
    return rms_norm(x, final_norm_g)
```

```python
import functools

import numpy as np
import jax
import jax.numpy as jnp
from jax import lax
from jax.experimental import pallas as pl
from jax.experimental.pallas import tpu as pltpu

F32 = jnp.float32
BF16 = jnp.bfloat16
I32 = jnp.int32

D_MODEL = 2048
GLA_HEADS = 4
GLA_DK = 128
GLA_DV = 256
GLA_RANK = 16
GLA_TAU = 16.0
GLA_CHUNK = 64
GLA_QK = GLA_HEADS * GLA_DK
GLA_V = GLA_HEADS * GLA_DV
ATT_HEADS = 16
ATT_HD = 64
ATT_W = ATT_HEADS * ATT_HD
ATT_DILATIONS = (1, 4, 16)
ATT_RADIUS = 64
PEER_NKEYS = 128
PEER_HEADS = 8
PEER_TOPK = 16
PEER_EXPERTS = PEER_NKEYS * PEER_NKEYS
PEER_HALF = 128
EPS = 1e-6
NEG = -1e30
MAIN_W = 2 * GLA_QK + 2 * GLA_V + 3 * ATT_W
GD_W = 2 * GLA_RANK

LANES = 128
SUBLANES = 8
VMEM_LIMIT = 56 * 1024 * 1024

NT = (((1,), (1,)), ((), ()))
TN = (((0,), (0,)), ((), ()))


def _cparams(n_axes):
    return pltpu.CompilerParams(dimension_semantics=("arbitrary",) * n_axes,
                                vmem_limit_bytes=VMEM_LIMIT)


def _rms(x, g):
    ms = jnp.mean(x * x, axis=-1, keepdims=True)
    return x * lax.rsqrt(ms + EPS) * g


def _inproj_body(x_ref, g_ref, w_ref, wgd_ref, o_ref, gd_ref, xn_ref):
    @pl.when(pl.program_id(1) == 0)
    def _():
        xn = _rms(x_ref[...], g_ref[...]).astype(BF16)
        xn_ref[...] = xn
        gd_ref[...] = jnp.dot(xn, wgd_ref[...], preferred_element_type=F32)

    o_ref[...] = jnp.dot(xn_ref[...], w_ref[...], preferred_element_type=F32)


def _inproj(x2, g, w_main, w_gd, tm=1024, tn=1024):
    T = x2.shape[0]
    return pl.pallas_call(
        _inproj_body,
        out_shape=(jax.ShapeDtypeStruct((T, MAIN_W), F32), jax.ShapeDtypeStruct((T, GD_W), F32)),
        grid=(T // tm, MAIN_W // tn),
        in_specs=[pl.BlockSpec((tm, D_MODEL), lambda i, j: (i, 0)),
                  pl.BlockSpec((1, D_MODEL), lambda i, j: (0, 0)),
                  pl.BlockSpec((D_MODEL, tn), lambda i, j: (0, j)),
                  pl.BlockSpec((D_MODEL, GD_W), lambda i, j: (0, 0))],
        out_specs=(pl.BlockSpec((tm, tn), lambda i, j: (i, j)),
                   pl.BlockSpec((tm, GD_W), lambda i, j: (i, 0))),
        scratch_shapes=[pltpu.VMEM((tm, D_MODEL), BF16)],
        compiler_params=_cparams(2),
        name="inproj",
    )(x2, g, w_main, w_gd)


def _log_gate(gd, up_ref, bias_ref):
    z = jnp.dot(gd.astype(BF16), up_ref[...].astype(BF16), preferred_element_type=F32) + bias_ref[...]
    return (jnp.minimum(z, 0.0) - jnp.log1p(jnp.exp(-jnp.abs(z)))) * (1.0 / GLA_TAU)


def _gla_chunk(q_ref, k_ref, v_ref, la_ref, st_ref, o_ref, r0, tri_mask, tot_row):
    C = GLA_CHUNK
    sl = pl.ds(pl.multiple_of(r0, C), C)
    la = la_ref[sl, :]
    hi = la.astype(BF16)
    lo = (la - hi.astype(F32)).astype(BF16)
    tri = jnp.where(tri_mask, 1.0, 0.0).astype(BF16)
    b = (jnp.dot(tri, hi, preferred_element_type=F32) + jnp.dot(tri, lo, preferred_element_type=F32))
    btot = b[tot_row:tot_row + 1, :]
    q = q_ref[sl, :]
    k = k_ref[sl, :]
    v = v_ref[sl, :].astype(BF16)
    qd = (q * (jnp.exp(b) * (GLA_DK ** -0.5))).astype(BF16)
    kd = (k * jnp.exp(-b)).astype(BF16)
    kr = (k * jnp.exp(btot - b)).astype(BF16)
    att = lax.dot_general(qd, kd, NT, preferred_element_type=F32)
    att = jnp.where(tri_mask, att, 0.0).astype(BF16)
    st = st_ref[...]
    o = (jnp.dot(att, v, preferred_element_type=F32)
         + lax.dot_general(qd, st.astype(BF16), NT, preferred_element_type=F32))
    o_ref[sl, :] = o
    st_ref[...] = jnp.exp(btot) * st + lax.dot_general(v, kr, TN, preferred_element_type=F32)


def _gla_body(qf, kf, vf, gdf, qb, kb, vb, gdb, upf, bsf, upb, bsb, of_ref, ob_ref,
              stf, stb, laf, lab, *, nchunk):
    @pl.when(pl.program_id(2) == 0)
    def _():
        stf[...] = jnp.zeros_like(stf)
        stb[...] = jnp.zeros_like(stb)

    laf[...] = _log_gate(gdf[:, 0:GLA_RANK], upf, bsf)
    lab[...] = _log_gate(gdb[:, GLA_RANK:2 * GLA_RANK], upb, bsb)
    C = GLA_CHUNK
    row = lax.broadcasted_iota(I32, (C, C), 0)
    col = lax.broadcasted_iota(I32, (C, C), 1)
    lower = row >= col
    upper = col >= row

    def step(c, carry):
        _gla_chunk(qf, kf, vf, laf, stf, of_ref, c * C, lower, C - 1)
        _gla_chunk(qb, kb, vb, lab, stb, ob_ref, (nchunk - 1 - c) * C, upper, 0)
        return carry

    lax.fori_loop(0, nchunk, step, 0)


def _gla(main3, gd3, up_f, bias_f, up_b, bias_b, lb=512):
    B, S, _ = main3.shape
    nb = S // lb
    kq = GLA_QK // GLA_DK
    kv = (2 * GLA_QK) // GLA_DV
    fwd = lambda b, h, i: (b, i, h)
    blk = lambda c0: (lambda b, h, i: (b, i, c0 + h))
    rblk = lambda c0: (lambda b, h, i: (b, nb - 1 - i, c0 + h))
    sq = lambda m: pl.BlockSpec((None, lb, GLA_DK), m)
    sv = lambda m: pl.BlockSpec((None, lb, GLA_DV), m)
    sg = lambda m: pl.BlockSpec((None, lb, GD_W), m)
    up_spec = pl.BlockSpec((GLA_RANK, GLA_DK), lambda b, h, i: (0, h))
    bias_spec = pl.BlockSpec((1, GLA_DK), lambda b, h, i: (0, h))
    out_sds = jax.ShapeDtypeStruct((B, S, GLA_V), F32)
    return pl.pallas_call(
        functools.partial(_gla_body, nchunk=lb // GLA_CHUNK),
        out_shape=(out_sds, out_sds),
        grid=(B, GLA_HEADS, nb),
        in_specs=[sq(blk(0)), sq(blk(kq)), sv(blk(kv)), sg(lambda b, h, i: (b, i, 0)),
                  sq(rblk(0)), sq(rblk(kq)), sv(rblk(kv)), sg(lambda b, h, i: (b, nb - 1 - i, 0)),
                  up_spec, bias_spec, up_spec, bias_spec],
        out_specs=(pl.BlockSpec((None, lb, GLA_DV), fwd),
                   pl.BlockSpec((None, lb, GLA_DV), lambda b, h, i: (b, nb - 1 - i, h))),
        scratch_shapes=[pltpu.VMEM((GLA_DV, GLA_DK), F32), pltpu.VMEM((GLA_DV, GLA_DK), F32),
                        pltpu.VMEM((lb, GLA_DK), F32), pltpu.VMEM((lb, GLA_DK), F32)],
        compiler_params=_cparams(3),
        name="gla",
    )(main3, main3, main3, gd3, main3, main3, main3, gd3, up_f, bias_f, up_b, bias_b)


ATT_TQ = 128
ATT_WIN = 256


def _attn_body(slopes_ref, q_ref, k_ref, v_ref, o_ref, m_s, l_s, acc_s, *, seq):
    pair = pl.program_id(1)
    lane = lax.broadcasted_iota(I32, (1, LANES), 1)
    head0 = lane < ATT_HD
    ii = lax.broadcasted_iota(I32, (ATT_TQ, ATT_WIN), 0)
    jj = lax.broadcasted_iota(I32, (ATT_TQ, ATT_WIN), 1)
    slope = (slopes_ref[2 * pair], slopes_ref[2 * pair + 1])

    for pi, d in enumerate(ATT_DILATIONS):
        L = seq // d
        nblk = L // ATT_TQ

        def block(t, carry, d=d, L=L, nblk=nblk, first=(pi == 0)):
            r = t // nblk
            q0 = (t % nblk) * ATT_TQ
            start = jnp.clip(q0 - ATT_RADIUS, 0, L - ATT_WIN)
            ad = jnp.abs(jj - ii - (q0 - start))
            valid = ad <= ATT_RADIUS
            dist = (ad * d).astype(F32)
            if d == 1:
                qsl = pl.ds(q0, ATT_TQ)
                ksl = pl.ds(start, ATT_WIN)
            else:
                qsl = pl.ds(r + d * q0, ATT_TQ, stride=d)
                ksl = pl.ds(r + d * start, ATT_WIN, stride=d)
            q = q_ref[qsl, :] * (ATT_HD ** -0.5)
            k = k_ref[ksl, :].astype(BF16)
            v = v_ref[ksl, :].astype(BF16)
            if not first:
                m_run = m_s[qsl, :]
            m_new, lsum, pv = [], [], []
            for hh in range(2):
                hm = head0 if hh == 0 else jnp.logical_not(head0)
                qh = jnp.where(hm, q, 0.0).astype(BF16)
                s = lax.dot_general(qh, k, NT, preferred_element_type=F32)
                s = jnp.where(valid, s - slope[hh] * dist, NEG)
                mh = jnp.max(s, axis=-1, keepdims=True)
                if not first:
                    mh = jnp.maximum(mh, m_run[:, hh * ATT_HD:hh * ATT_HD + 1])
                p = jnp.exp(s - mh)
                m_new.append(mh)
                lsum.append(jnp.sum(p, axis=-1, keepdims=True))
                pv.append(jnp.dot(p.astype(BF16), v, preferred_element_type=F32))
            m_full = jnp.where(head0, m_new[0], m_new[1])
            l_blk = jnp.where(head0, lsum[0], lsum[1])
            pv_blk = jnp.where(head0, pv[0], pv[1])
            if first:
                l_s[qsl, :] = l_blk
                acc_s[qsl, :] = pv_blk
            else:
                alpha = jnp.exp(m_run - m_full)
                l_s[qsl, :] = alpha * l_s[qsl, :] + l_blk
                acc_s[qsl, :] = alpha * acc_s[qsl, :] + pv_blk
            m_s[qsl, :] = m_full
            return carry

        lax.fori_loop(0, d * nblk, block, 0)

    o_ref[...] = acc_s[...] / l_s[...]


def _attn(main3, slopes):
    B, S, _ = main3.shape
    c0 = (2 * GLA_QK + 2 * GLA_V) // LANES
    npair = ATT_HEADS // 2
    spec = lambda off: pl.BlockSpec((None, S, LANES), lambda b, p: (b, 0, c0 + off + p))
    return pl.pallas_call(
        functools.partial(_attn_body, seq=S),
        out_shape=jax.ShapeDtypeStruct((B, S, ATT_W), F32),
        grid=(B, npair),
        in_specs=[pl.BlockSpec(memory_space=pltpu.SMEM), spec(0), spec(npair), spec(2 * npair)],
        out_specs=pl.BlockSpec((None, S, LANES), lambda b, p: (b, 0, p)),
        scratch_shapes=[pltpu.VMEM((S, LANES), F32)] * 3,
        compiler_params=_cparams(2),
        name="attn",
    )(slopes, main3, main3, main3)


def _mix_body(of_ref, ob_ref, gr_ref, att_ref, x_ref, gg_ref, ag_ref, wo_ref, n2_ref, h_ref, hn_ref):
    o = of_ref[...] + ob_ref[...]
    parts = [_rms(o[:, h * GLA_DV:(h + 1) * GLA_DV], gg_ref[...]) for h in range(GLA_HEADS)]
    on = jnp.concatenate(parts, axis=-1)
    gr = gr_ref[...]
    gla = on * (gr / (1.0 + jnp.exp(-gr)))
    an = _rms(att_ref[...], ag_ref[...])
    y = (jnp.dot(gla.astype(BF16), wo_ref[0:GLA_V, :], preferred_element_type=F32)
         + jnp.dot(an.astype(BF16), wo_ref[GLA_V:GLA_V + ATT_W, :], preferred_element_type=F32))
    h = x_ref[...] + y
    h_ref[...] = h
    hn_ref[...] = _rms(h, n2_ref[...]).astype(BF16)


def _mix(o_f, o_b, main2, att2, x2, gg, ag, w_out, n2g, tm=512):
    T = x2.shape[0]
    gr_blk = (2 * GLA_QK + GLA_V) // GLA_V
    row = lambda i: (i, 0)
    const = lambda i: (0, 0)
    return pl.pallas_call(
        _mix_body,
        out_shape=(jax.ShapeDtypeStruct((T, D_MODEL), F32), jax.ShapeDtypeStruct((T, D_MODEL), BF16)),
        grid=(T // tm,),
        in_specs=[pl.BlockSpec((tm, GLA_V), row), pl.BlockSpec((tm, GLA_V), row),
                  pl.BlockSpec((tm, GLA_V), lambda i: (i, gr_blk)),
                  pl.BlockSpec((tm, ATT_W), row), pl.BlockSpec((tm, D_MODEL), row),
                  pl.BlockSpec((1, GLA_DV), const), pl.BlockSpec((1, ATT_W), const),
                  pl.BlockSpec((D_MODEL, D_MODEL), const), pl.BlockSpec((1, D_MODEL), const)],
        out_specs=(pl.BlockSpec((tm, D_MODEL), row), pl.BlockSpec((tm, D_MODEL), row)),
        compiler_params=_cparams(1),
        name="mix",
    )(o_f, o_b, main2, att2, x2, gg, ag, w_out, n2g)


ROUTE_TM = SUBLANES * LANES
_CELLS = [(a, b) for a in range(PEER_TOPK) for b in range(PEER_TOPK) if (a + 1) * (b + 1) <= PEER_TOPK]


def _key_rows(i):
    return pl.ds(i * SUBLANES, SUBLANES)


def _rank_keys(s_ref, rank_ref, bacc_ref):
    n = PEER_NKEYS
    G = SUBLANES
    bacc_ref[...] = jnp.zeros_like(bacc_ref)
    for ib in range(n // G):
        vi = [s_ref[_key_rows(ib * G + a), :] for a in range(G)]
        wins = [jnp.zeros((SUBLANES, LANES), F32) for _ in range(G)]
        lost = [jnp.zeros((SUBLANES, LANES), F32) for _ in range(G)]
        for a in range(G):
            for c in range(a + 1, G):
                bt = jnp.where(vi[a] >= vi[c], 1.0, 0.0)
                wins[a] = wins[a] + bt
                lost[c] = lost[c] + bt

        def body(j, ws, vi=vi):
            vj = s_ref[_key_rows(j), :]
            bj = bacc_ref[_key_rows(j), :]
            ws = list(ws)
            for a in range(G):
                bt = jnp.where(vi[a] >= vj, 1.0, 0.0)
                ws[a] = ws[a] + bt
                bj = bj + bt
            bacc_ref[_key_rows(j), :] = bj
            return tuple(ws)

        lo = (ib + 1) * G
        if lo < n:
            wins = list(lax.fori_loop(lo, n, body, tuple(wins), unroll=4))
        for a in range(G):
            i = ib * G + a
            later = float(n - 1 - i)
            rank_ref[_key_rows(i), :] = (later - wins[a]) + lost[a] + bacc_ref[_key_rows(i), :]


def _sorted_top(s_ref, rank_ref):
    def body(i, acc):
        v = s_ref[_key_rows(i), :]
        rk = rank_ref[_key_rows(i), :]
        return tuple(acc[k] + jnp.where(rk == float(k), v, 0.0) for k in range(PEER_TOPK))

    zero = jnp.zeros((SUBLANES, LANES), F32)
    return lax.fori_loop(0, PEER_NKEYS, body, (zero,) * PEER_TOPK)


def _to_token_lanes(src_ref, dst_ref):
    for g in range(SUBLANES):
        dst_ref[:, g * LANES:(g + 1) * LANES] = src_ref[pl.ds(g, PEER_NKEYS, stride=SUBLANES), :]


def _route_body(hn_ref, wq_ref, keys_ref, r2_ref, e2_ref, n1_ref, c1_ref,
                s1_s, s2_s, rk1_s, rk2_s, bacc_s, tmp_s):
    q = jnp.dot(hn_ref[...], wq_ref[...], preferred_element_type=F32)
    for c, s_s in enumerate((s1_s, s2_s)):
        qc = q[:, c * PEER_HALF:(c + 1) * PEER_HALF].astype(BF16)
        st = lax.dot_general(keys_ref[c], qc, NT, preferred_element_type=F32)
        for g in range(SUBLANES):
            s_s[pl.ds(g, PEER_NKEYS, stride=SUBLANES), :] = st[:, g * LANES:(g + 1) * LANES]
    _rank_keys(s1_s, rk1_s, bacc_s)
    _rank_keys(s2_s, rk2_s, bacc_s)
    sc1 = _sorted_top(s1_s, rk1_s)
    sc2 = _sorted_top(s2_s, rk2_s)
    e1 = [jnp.exp(sc1[k] - sc1[0]) for k in range(PEER_TOPK)]
    e2 = [jnp.exp(sc2[k] - sc2[0]) for k in range(PEER_TOPK)]

    cand = [sc1[a] + sc2[b] for (a, b) in _CELLS]
    nc = len(_CELLS)
    before = [jnp.zeros((SUBLANES, LANES), F32) for _ in range(nc)]
    for x in range(nc):
        for y in range(x + 1, nc):
            ax, bx = _CELLS[x]
            ay, by = _CELLS[y]
            if ax <= ay and bx <= by:
                before[y] = before[y] + 1.0
                continue
            bt = jnp.where(cand[x] >= cand[y], 1.0, 0.0)
            before[y] = before[y] + bt
            before[x] = before[x] + (1.0 - bt)
    zero = jnp.zeros((SUBLANES, LANES), F32)
    ncol = [zero] * PEER_TOPK
    zsum = zero
    for x, (a, b) in enumerate(_CELLS):
        sel = jnp.where(before[x] < float(PEER_TOPK), 1.0, 0.0)
        ncol[a] = ncol[a] + sel
        zsum = zsum + sel * (e1[a] * e2[b])
    inv_z = 1.0 / zsum

    def dense(i, carry):
        rows = _key_rows(i)
        rk1 = rk1_s[rows, :]
        n1 = zero
        for k in range(PEER_TOPK):
            n1 = n1 + jnp.where(rk1 == float(k), ncol[k], 0.0)
        bacc_s[rows, :] = n1
        s1_s[rows, :] = jnp.exp(s1_s[rows, :] - sc1[0]) * inv_z
        s2_s[rows, :] = jnp.exp(s2_s[rows, :] - sc2[0])
        return carry

    lax.fori_loop(0, PEER_NKEYS, dense, 0)
    _to_token_lanes(rk2_s, r2_ref)
    _to_token_lanes(s2_s, e2_ref)
    _to_token_lanes(bacc_s, n1_ref)
    _to_token_lanes(s1_s, c1_ref)


def _route(hn, w_q, keys):
    T = hn.shape[0]
    tm = ROUTE_TM
    out_sds = jax.ShapeDtypeStruct((PEER_HEADS, PEER_NKEYS, T), F32)
    out_spec = pl.BlockSpec((None, PEER_NKEYS, tm), lambda i, h: (h, 0, i))
    km = pltpu.VMEM((PEER_NKEYS * SUBLANES, LANES), F32)
    return pl.pallas_call(
        _route_body,
        out_shape=(out_sds,) * 4,
        grid=(T // tm, PEER_HEADS),
        in_specs=[pl.BlockSpec((tm, D_MODEL), lambda i, h: (i, 0)),
                  pl.BlockSpec((D_MODEL, 2 * PEER_HALF), lambda i, h: (0, h)),
                  pl.BlockSpec((None, 2, PEER_NKEYS, PEER_HALF), lambda i, h: (h, 0, 0, 0))],
        out_specs=(out_spec,) * 4,
        scratch_shapes=[km] * 6,
        compiler_params=_cparams(2),
        name="route",
    )(hn, w_q, keys)


PEER_TE = 512
INV_SQRT2 = 0.7071067811865476


def _peer_body(hn_ref, u_ref, vt_ref, r2_ref, e2_ref, n1_ref, c1_ref, o_ref):
    j = pl.program_id(1)

    @pl.when(j == 0)
    def _():
        o_ref[...] = jnp.zeros_like(o_ref)

    a = lax.dot_general(u_ref[...], hn_ref[...], NT, preferred_element_type=F32)
    act = 0.5 * a * (1.0 + lax.erf(a * INV_SQRT2))
    rows = []
    for il in range(PEER_TE // PEER_NKEYS):
        i1 = j * (PEER_TE // PEER_NKEYS) + il
        w = None
        for h in range(PEER_HEADS):
            n1 = n1_ref[h, pl.ds(i1, 1), :]
            c1 = c1_ref[h, pl.ds(i1, 1), :]
            t = jnp.where(r2_ref[h] < n1, e2_ref[h], 0.0) * c1
            w = t if w is None else w + t
        rows.append(w)
    wgt = jnp.concatenate(rows, axis=0)
    g = (act * wgt).astype(BF16)
    o_ref[...] += jnp.dot(vt_ref[...], g, preferred_element_type=F32)


def _peer(hn, u, vt, r2, e2, n1, c1, tm=512):
    T = hn.shape[0]
    te = PEER_TE
    rspec = pl.BlockSpec((PEER_HEADS, PEER_NKEYS, tm), lambda i, j: (0, 0, i))
    return pl.pallas_call(
        _peer_body,
        out_shape=jax.ShapeDtypeStruct((D_MODEL, T), F32),
        grid=(T // tm, PEER_EXPERTS // te),
        in_specs=[pl.BlockSpec((tm, D_MODEL), lambda i, j: (i, 0)),
                  pl.BlockSpec((te, D_MODEL), lambda i, j: (j, 0)),
                  pl.BlockSpec((D_MODEL, te), lambda i, j: (0, j)),
                  rspec, rspec, rspec, rspec],
        out_specs=pl.BlockSpec((D_MODEL, tm), lambda i, j: (0, i)),
        compiler_params=_cparams(2),
        name="peer",
    )(hn, u, vt, r2, e2, n1, c1)


def _final_body(h_ref, pt_ref, g_ref, o_ref):
    o_ref[...] = _rms(h_ref[...] + pt_ref[...].T, g_ref[...])


def _final(h, peer_t, g, tm=512):
    T = h.shape[0]
    return pl.pallas_call(
        _final_body,
        out_shape=jax.ShapeDtypeStruct((T, D_MODEL), F32),
        grid=(T // tm,),
        in_specs=[pl.BlockSpec((tm, D_MODEL), lambda i: (i, 0)),
                  pl.BlockSpec((D_MODEL, tm), lambda i: (0, i)),
                  pl.BlockSpec((1, D_MODEL), lambda i: (0, 0))],
        out_specs=pl.BlockSpec((tm, D_MODEL), lambda i: (i, 0)),
        compiler_params=_cparams(1),
        name="final",
    )(h, peer_t, g)


def _layer(x, norm1_g, w_in, up_f, bias_f, up_b, bias_b, gla_norm_g, att_norm_g, w_out, norm2_g,
           w_q, sub_keys, peer_u, peer_v):
    B, S, D = x.shape
    T = B * S
    x2 = x.reshape(T, D)
    gd0 = 2 * GLA_QK + 2 * GLA_V
    w_main = jnp.concatenate([w_in[:, :gd0], w_in[:, gd0 + GD_W:]], axis=1).astype(BF16)
    w_gd = w_in[:, gd0:gd0 + GD_W].astype(BF16)
    main, gd = _inproj(x2, norm1_g.reshape(1, D), w_main, w_gd)
    main3 = main.reshape(B, S, MAIN_W)
    o_f, o_b = _gla(main3, gd.reshape(B, S, GD_W), up_f, bias_f.reshape(1, GLA_QK),
                    up_b, bias_b.reshape(1, GLA_QK))
    slopes = jnp.asarray((2.0 ** (-8.0 * np.arange(1, ATT_HEADS + 1) / ATT_HEADS)).astype(np.float32))
    att = _attn(main3, slopes)
    h, hn = _mix(o_f.reshape(T, GLA_V), o_b.reshape(T, GLA_V), main, att.reshape(T, ATT_W), x2,
                 gla_norm_g.reshape(1, GLA_DV), att_norm_g.reshape(1, ATT_W), w_out.astype(BF16),
                 norm2_g.reshape(1, D))
    r2, e2, n1, c1 = _route(hn, w_q.astype(BF16), sub_keys.astype(BF16))
    peer_t = _peer(hn, peer_u.astype(BF16), peer_v.astype(BF16).T, r2, e2, n1, c1)
    return h, peer_t


def kernel(x, norm1_g, w_in, gla_gate_up_f, gla_gate_bias_f, gla_gate_up_b, gla_gate_bias_b, gla_norm_g,
           att_norm_g, w_out, norm2_g, peer_w_q, peer_sub_keys, peer_u, peer_v, final_norm_g):
    B, S, D = x.shape
    assert norm1_g.shape[0] == 1, "single trunk layer: the final norm is fused with the layer's last residual"
    h, peer_t = _layer(x, norm1_g[0], w_in[0], gla_gate_up_f[0], gla_gate_bias_f[0], gla_gate_up_b[0],
                       gla_gate_bias_b[0], gla_norm_g[0], att_norm_g[0], w_out[0], norm2_g[0],
                       peer_w_q[0], peer_sub_keys[0], peer_u[0], peer_v[0])
    return _final(h, peer_t, final_norm_g.reshape(1, D)).reshape(B, S, D)
```

```python
import functools

import numpy as np
import jax
import jax.numpy as jnp
from jax import lax
from jax.experimental import pallas as pl
from jax.experimental.pallas import tpu as pltpu

F32 = jnp.float32
BF16 = jnp.bfloat16
I32 = jnp.int32

D_MODEL = 2048
GLA_HEADS = 4
GLA_DK = 128
GLA_DV = 256
GLA_RANK = 16
GLA_TAU = 16.0
GLA_CHUNK = 64
GLA_QK = GLA_HEADS * GLA_DK
GLA_V = GLA_HEADS * GLA_DV
ATT_HEADS = 16
ATT_HD = 64
ATT_W = ATT_HEADS * ATT_HD
ATT_DILATIONS = (1, 4, 16)
ATT_RADIUS = 64
PEER_NKEYS = 128
PEER_HEADS = 8
PEER_TOPK = 16
PEER_EXPERTS = PEER_NKEYS * PEER_NKEYS
PEER_HALF = 128
EPS = 1e-6
NEG = -1e30
MAIN_W = 2 * GLA_QK + 2 * GLA_V + 3 * ATT_W
GD_W = 2 * GLA_RANK

LANES = 128
SUBLANES = 8
VMEM_LIMIT = 56 * 1024 * 1024

NT = (((1,), (1,)), ((), ()))
TN = (((0,), (0,)), ((), ()))


def _cparams(n_axes, flags=None):
    return pltpu.CompilerParams(dimension_semantics=("arbitrary",) * n_axes,
                                vmem_limit_bytes=VMEM_LIMIT, flags=flags)


def _rms(x, g):
    ms = jnp.mean(x * x, axis=-1, keepdims=True)
    return x * lax.rsqrt(ms + EPS) * g


def _inproj_body(x_ref, g_ref, w_ref, wgd_ref, o_ref, gd_ref, xn_ref):
    @pl.when(pl.program_id(1) == 0)
    def _():
        xn = _rms(x_ref[...], g_ref[...]).astype(BF16)
        xn_ref[...] = xn
        gd_ref[...] = jnp.dot(xn, wgd_ref[...], preferred_element_type=F32)

    o_ref[...] = jnp.dot(xn_ref[...], w_ref[...], preferred_element_type=F32)


def _inproj(x2, g, w_main, w_gd, tm=1024, tn=1024):
    T = x2.shape[0]
    return pl.pallas_call(
        _inproj_body,
        out_shape=(jax.ShapeDtypeStruct((T, MAIN_W), F32), jax.ShapeDtypeStruct((T, GD_W), F32)),
        grid=(T // tm, MAIN_W // tn),
        in_specs=[pl.BlockSpec((tm, D_MODEL), lambda i, j: (i, 0)),
                  pl.BlockSpec((1, D_MODEL), lambda i, j: (0, 0)),
                  pl.BlockSpec((D_MODEL, tn), lambda i, j: (0, j)),
                  pl.BlockSpec((D_MODEL, GD_W), lambda i, j: (0, 0))],
        out_specs=(pl.BlockSpec((tm, tn), lambda i, j: (i, j)),
                   pl.BlockSpec((tm, GD_W), lambda i, j: (i, 0))),
        scratch_shapes=[pltpu.VMEM((tm, D_MODEL), BF16)],
        compiler_params=_cparams(2),
        name="inproj",
    )(x2, g, w_main, w_gd)


def _log_gate(gd, up_ref, bias_ref):
    z = jnp.dot(gd.astype(BF16), up_ref[...].astype(BF16), preferred_element_type=F32) + bias_ref[...]
    return (jnp.minimum(z, 0.0) - jnp.log1p(jnp.exp(-jnp.abs(z)))) * (1.0 / GLA_TAU)


def _gla_chunk(q_ref, k_ref, v_ref, la_ref, st_ref, o_ref, r0, tri_mask, tot_row):
    C = GLA_CHUNK
    sl = pl.ds(pl.multiple_of(r0, C), C)
    la = la_ref[sl, :]
    hi = la.astype(BF16)
    lo = (la - hi.astype(F32)).astype(BF16)
    tri = jnp.where(tri_mask, 1.0, 0.0).astype(BF16)
    b = (jnp.dot(tri, hi, preferred_element_type=F32) + jnp.dot(tri, lo, preferred_element_type=F32))
    btot = b[tot_row:tot_row + 1, :]
    q = q_ref[sl, :]
    k = k_ref[sl, :]
    v = v_ref[sl, :].astype(BF16)
    qd = (q * (jnp.exp(b) * (GLA_DK ** -0.5))).astype(BF16)
    kd = (k * jnp.exp(-b)).astype(BF16)
    kr = (k * jnp.exp(btot - b)).astype(BF16)
    att = lax.dot_general(qd, kd, NT, preferred_element_type=F32)
    att = jnp.where(tri_mask, att, 0.0).astype(BF16)
    st = st_ref[...]
    o = (jnp.dot(att, v, preferred_element_type=F32)
         + lax.dot_general(qd, st.astype(BF16), NT, preferred_element_type=F32))
    o_ref[sl, :] = o
    st_ref[...] = jnp.exp(btot) * st + lax.dot_general(v, kr, TN, preferred_element_type=F32)


def _gla_body(qf, kf, vf, gdf, qb, kb, vb, gdb, upf, bsf, upb, bsb, of_ref, ob_ref,
              stf, stb, laf, lab, *, nchunk):
    @pl.when(pl.program_id(2) == 0)
    def _():
        stf[...] = jnp.zeros_like(stf)
        stb[...] = jnp.zeros_like(stb)

    laf[...] = _log_gate(gdf[:, 0:GLA_RANK], upf, bsf)
    lab[...] = _log_gate(gdb[:, GLA_RANK:2 * GLA_RANK], upb, bsb)
    C = GLA_CHUNK
    row = lax.broadcasted_iota(I32, (C, C), 0)
    col = lax.broadcasted_iota(I32, (C, C), 1)
    lower = row >= col
    upper = col >= row

    def step(c, carry):
        _gla_chunk(qf, kf, vf, laf, stf, of_ref, c * C, lower, C - 1)
        _gla_chunk(qb, kb, vb, lab, stb, ob_ref, (nchunk - 1 - c) * C, upper, 0)
        return carry

    lax.fori_loop(0, nchunk, step, 0)


def _gla(main3, gd3, up_f, bias_f, up_b, bias_b, lb=512):
    B, S, _ = main3.shape
    nb = S // lb
    kq = GLA_QK // GLA_DK
    kv = (2 * GLA_QK) // GLA_DV
    fwd = lambda b, h, i: (b, i, h)
    blk = lambda c0: (lambda b, h, i: (b, i, c0 + h))
    rblk = lambda c0: (lambda b, h, i: (b, nb - 1 - i, c0 + h))
    sq = lambda m: pl.BlockSpec((None, lb, GLA_DK), m)
    sv = lambda m: pl.BlockSpec((None, lb, GLA_DV), m)
    sg = lambda m: pl.BlockSpec((None, lb, GD_W), m)
    up_spec = pl.BlockSpec((GLA_RANK, GLA_DK), lambda b, h, i: (0, h))
    bias_spec = pl.BlockSpec((1, GLA_DK), lambda b, h, i: (0, h))
    out_sds = jax.ShapeDtypeStruct((B, S, GLA_V), F32)
    return pl.pallas_call(
        functools.partial(_gla_body, nchunk=lb // GLA_CHUNK),
        out_shape=(out_sds, out_sds),
        grid=(B, GLA_HEADS, nb),
        in_specs=[sq(blk(0)), sq(blk(kq)), sv(blk(kv)), sg(lambda b, h, i: (b, i, 0)),
                  sq(rblk(0)), sq(rblk(kq)), sv(rblk(kv)), sg(lambda b, h, i: (b, nb - 1 - i, 0)),
                  up_spec, bias_spec, up_spec, bias_spec],
        out_specs=(pl.BlockSpec((None, lb, GLA_DV), fwd),
                   pl.BlockSpec((None, lb, GLA_DV), lambda b, h, i: (b, nb - 1 - i, h))),
        scratch_shapes=[pltpu.VMEM((GLA_DV, GLA_DK), F32), pltpu.VMEM((GLA_DV, GLA_DK), F32),
                        pltpu.VMEM((lb, GLA_DK), F32), pltpu.VMEM((lb, GLA_DK), F32)],
        compiler_params=_cparams(3),
        name="gla",
    )(main3, main3, main3, gd3, main3, main3, main3, gd3, up_f, bias_f, up_b, bias_b)


ATT_TQ = 128
ATT_WIN = 256


ATT_GROUP = 4


ATT_FACTOR = 4
assert all(b == a * ATT_FACTOR for a, b in zip(ATT_DILATIONS[:-1], ATT_DILATIONS[1:])) and ATT_DILATIONS[0] == 1


def _split_classes(src, dst, d, n):
    f = ATT_FACTOR
    sub = n // (d * f)
    for r in range(d):
        for a in range(f):
            dst[pl.ds((d * a + r) * sub, sub), :] = src[pl.ds(r * (n // d) + a, sub, stride=f), :]


def _merge_classes(src, dst, d, n):
    f = ATT_FACTOR
    sub = n // (d * f)
    for r in range(d):
        for a in range(f):
            dst[pl.ds(r * (n // d) + a, sub, stride=f), :] = src[pl.ds((d * a + r) * sub, sub), :]


def _attn_body(slopes_ref, q_ref, k_ref, v_ref, o_ref, *scr, seq):
    pair = pl.program_id(1)
    lane = lax.broadcasted_iota(I32, (1, LANES), 1)
    head0 = lane < ATT_HD
    ii = lax.broadcasted_iota(I32, (ATT_TQ, ATT_WIN), 0)
    jj = lax.broadcasted_iota(I32, (ATT_TQ, ATT_WIN), 1)
    slope = (slopes_ref[2 * pair], slopes_ref[2 * pair + 1])
    npat = len(ATT_DILATIONS)
    copies = [(q_ref, k_ref, v_ref)] + [scr[3 * i:3 * i + 3] for i in range(npat - 1)]
    stats = [scr[3 * (npat - 1):3 * npat], scr[3 * npat:3 * (npat + 1)]]
    bias_s = scr[3 * (npat + 1)]
    for pi in range(1, npat):
        for x in range(3):
            _split_classes(copies[pi - 1][x], copies[pi][x], ATT_DILATIONS[pi - 1], seq)

    for step, pi in enumerate(reversed(range(npat))):
        d = ATT_DILATIONS[pi]
        L = seq // d
        nblk = L // ATT_TQ
        first = step == 0
        q_ref, k_ref, v_ref = copies[pi]
        if not first:
            for x in range(3):
                _merge_classes(stats[(step - 1) % 2][x], stats[step % 2][x], d, seq)
        m_s, l_s, acc_s = stats[step % 2]
        for case in range(3):
            ad = jnp.abs(jj - ii - case * ATT_RADIUS)
            for hh in range(2):
                bias_s[case, hh] = jnp.where(ad <= ATT_RADIUS, -slope[hh] * (ad * d).astype(F32), NEG)

        def group(gi, carry, L=L, nblk=nblk, first=first, q_ref=q_ref, k_ref=k_ref, v_ref=v_ref,
                  m_s=m_s, l_s=l_s, acc_s=acc_s):
            loaded = []
            for u in range(ATT_GROUP):
                t = gi * ATT_GROUP + u
                base = (t // nblk) * L
                q0 = (t % nblk) * ATT_TQ
                start = jnp.clip(q0 - ATT_RADIUS, 0, L - ATT_WIN)
                case = (q0 - start) // ATT_RADIUS
                qsl = pl.ds(pl.multiple_of(base + q0, ATT_TQ), ATT_TQ)
                ksl = pl.ds(pl.multiple_of(base + start, ATT_RADIUS), ATT_WIN)
                q = q_ref[qsl, :] * (ATT_HD ** -0.5)
                k = k_ref[ksl, :].astype(BF16)
                v = v_ref[ksl, :].astype(BF16)
                run = None if first else (m_s[qsl, :], l_s[qsl, :], acc_s[qsl, :])
                loaded.append((qsl, case, q, k, v, run))
            results = []
            for qsl, case, q, k, v, run in loaded:
                m_new, lsum, pv = [], [], []
                for hh in range(2):
                    hm = head0 if hh == 0 else jnp.logical_not(head0)
                    qh = jnp.where(hm, q, 0.0).astype(BF16)
                    s = lax.dot_general(qh, k, NT, preferred_element_type=F32) + bias_s[case, hh]
                    mh = jnp.max(s, axis=-1, keepdims=True)
                    p = jnp.exp(s - mh)
                    m_new.append(mh)
                    lsum.append(jnp.sum(p, axis=-1, keepdims=True))
                    pv.append(jnp.dot(p.astype(BF16), v, preferred_element_type=F32))
                m_full = jnp.where(head0, m_new[0], m_new[1])
                l_blk = jnp.where(head0, lsum[0], lsum[1])
                pv_blk = jnp.where(head0, pv[0], pv[1])
                if not first:
                    m_blk = m_full
                    m_full = jnp.maximum(run[0], m_blk)
                    a_run = jnp.exp(run[0] - m_full)
                    a_blk = jnp.exp(m_blk - m_full)
                    l_blk = a_run * run[1] + a_blk * l_blk
                    pv_blk = a_run * run[2] + a_blk * pv_blk
                results.append((qsl, m_full, l_blk, pv_blk))
            for qsl, m_full, l_blk, pv_blk in results:
                m_s[qsl, :] = m_full
                l_s[qsl, :] = l_blk
                acc_s[qsl, :] = pv_blk
            return carry

        lax.fori_loop(0, d * nblk // ATT_GROUP, group, 0)

    o_ref[...] = acc_s[...] / l_s[...]


def _attn(main3, slopes):
    B, S, _ = main3.shape
    c0 = (2 * GLA_QK + 2 * GLA_V) // LANES
    npair = ATT_HEADS // 2
    spec = lambda off: pl.BlockSpec((None, S, LANES), lambda b, p: (b, 0, c0 + off + p))
    return pl.pallas_call(
        functools.partial(_attn_body, seq=S),
        out_shape=jax.ShapeDtypeStruct((B, S, ATT_W), F32),
        grid=(B, npair),
        in_specs=[pl.BlockSpec(memory_space=pltpu.SMEM), spec(0), spec(npair), spec(2 * npair)],
        out_specs=pl.BlockSpec((None, S, LANES), lambda b, p: (b, 0, p)),
        scratch_shapes=([pltpu.VMEM((S, LANES), F32)] * (3 * (len(ATT_DILATIONS) + 1))
                        + [pltpu.VMEM((3, 2, ATT_TQ, ATT_WIN), F32)]),
        compiler_params=_cparams(2),
        name="attn",
    )(slopes, main3, main3, main3)


def _mix_body(of_ref, ob_ref, gr_ref, att_ref, x_ref, gg_ref, ag_ref, wo_ref, n2_ref, h_ref, hn_ref):
    o = of_ref[...] + ob_ref[...]
    parts = [_rms(o[:, h * GLA_DV:(h + 1) * GLA_DV], gg_ref[...]) for h in range(GLA_HEADS)]
    on = jnp.concatenate(parts, axis=-1)
    gr = gr_ref[...]
    gla = on * (gr / (1.0 + jnp.exp(-gr)))
    an = _rms(att_ref[...], ag_ref[...])
    y = (jnp.dot(gla.astype(BF16), wo_ref[0:GLA_V, :], preferred_element_type=F32)
         + jnp.dot(an.astype(BF16), wo_ref[GLA_V:GLA_V + ATT_W, :], preferred_element_type=F32))
    h = x_ref[...] + y
    h_ref[...] = h
    hn_ref[...] = _rms(h, n2_ref[...]).astype(BF16)


def _mix(o_f, o_b, main2, att2, x2, gg, ag, w_out, n2g, tm=512):
    T = x2.shape[0]
    gr_blk = (2 * GLA_QK + GLA_V) // GLA_V
    row = lambda i: (i, 0)
    const = lambda i: (0, 0)
    return pl.pallas_call(
        _mix_body,
        out_shape=(jax.ShapeDtypeStruct((T, D_MODEL), F32), jax.ShapeDtypeStruct((T, D_MODEL), BF16)),
        grid=(T // tm,),
        in_specs=[pl.BlockSpec((tm, GLA_V), row), pl.BlockSpec((tm, GLA_V), row),
                  pl.BlockSpec((tm, GLA_V), lambda i: (i, gr_blk)),
                  pl.BlockSpec((tm, ATT_W), row), pl.BlockSpec((tm, D_MODEL), row),
                  pl.BlockSpec((1, GLA_DV), const), pl.BlockSpec((1, ATT_W), const),
                  pl.BlockSpec((D_MODEL, D_MODEL), const), pl.BlockSpec((1, D_MODEL), const)],
        out_specs=(pl.BlockSpec((tm, D_MODEL), row), pl.BlockSpec((tm, D_MODEL), row)),
        compiler_params=_cparams(1),
        name="mix",
    )(o_f, o_b, main2, att2, x2, gg, ag, w_out, n2g)


ROUTE_TM = SUBLANES * LANES
_CELLS = [(a, b) for a in range(PEER_TOPK) for b in range(PEER_TOPK) if (a + 1) * (b + 1) <= PEER_TOPK]


def _key_rows(i):
    return pl.ds(i * SUBLANES, SUBLANES)


def _before(a, b):
    (va, ia), (vb, ib) = a, b
    if isinstance(ia, float) and isinstance(ib, float):
        return (va >= vb) if ia < ib else (va > vb)
    return jnp.logical_or(va > vb, jnp.logical_and(va == vb, ia < ib))


def _pick(c, a, b):
    return tuple(jnp.where(c, x, y) for x, y in zip(a, b))


def _ordered_pair(a, b):
    c = _before(a, b)
    return _pick(c, a, b), _pick(c, b, a)


def _bitonic_merge(xs):
    n = len(xs)
    if n == 1:
        return list(xs)
    h = n // 2
    xs = list(xs)
    for k in range(h):
        xs[k], xs[k + h] = _ordered_pair(xs[k], xs[k + h])
    return _bitonic_merge(xs[:h]) + _bitonic_merge(xs[h:])


def _bitonic_sort(xs):
    n = len(xs)
    if n == 1:
        return list(xs)
    return _bitonic_merge(_bitonic_sort(xs[:n // 2]) + _bitonic_sort(xs[n // 2:])[::-1])


def _top_keys(s_ref):
    k = PEER_TOPK
    best = None
    for g in range(PEER_NKEYS // k):
        grp = _bitonic_sort([(s_ref[_key_rows(g * k + a), :], float(g * k + a)) for a in range(k)])
        grp = [(v, jnp.full((SUBLANES, LANES), i, F32) if isinstance(i, float) else i) for v, i in grp]
        if best is None:
            best = grp
        else:
            best = _bitonic_merge([_pick(_before(best[j], grp[k - 1 - j]), best[j], grp[k - 1 - j])
                                   for j in range(k)])
    return [v for v, _ in best], [i for _, i in best]


def _to_token_lanes(src_ref, dst_ref):
    for g in range(SUBLANES):
        dst_ref[g] = src_ref[pl.ds(g, PEER_NKEYS, stride=SUBLANES), :]


def _route_body(hn_ref, wq_ref, keys_ref, r2_ref, e2_ref, n1_ref, c1_ref, s1_s, s2_s, rk2_s, n1_s):
    q = jnp.dot(hn_ref[...], wq_ref[...], preferred_element_type=F32)
    for c, s_s in enumerate((s1_s, s2_s)):
        qc = q[:, c * PEER_HALF:(c + 1) * PEER_HALF].astype(BF16)
        st = lax.dot_general(keys_ref[c], qc, NT, preferred_element_type=F32)
        for g in range(SUBLANES):
            s_s[pl.ds(g, PEER_NKEYS, stride=SUBLANES), :] = st[:, g * LANES:(g + 1) * LANES]
    sc1, si1 = _top_keys(s1_s)
    sc2, si2 = _top_keys(s2_s)
    e1 = [jnp.exp(sc1[k] - sc1[0]) for k in range(PEER_TOPK)]
    e2 = [jnp.exp(sc2[k] - sc2[0]) for k in range(PEER_TOPK)]

    cand = [sc1[a] + sc2[b] for (a, b) in _CELLS]
    nc = len(_CELLS)
    before = [jnp.zeros((SUBLANES, LANES), F32) for _ in range(nc)]
    for x in range(nc):
        for y in range(x + 1, nc):
            ax, bx = _CELLS[x]
            ay, by = _CELLS[y]
            if ax <= ay and bx <= by:
                before[y] = before[y] + 1.0
                continue
            bt = jnp.where(cand[x] >= cand[y], 1.0, 0.0)
            before[y] = before[y] + bt
            before[x] = before[x] + (1.0 - bt)
    zero = jnp.zeros((SUBLANES, LANES), F32)
    ncol = [zero] * PEER_TOPK
    zsum = zero
    for x, (a, b) in enumerate(_CELLS):
        sel = jnp.where(before[x] < float(PEER_TOPK), 1.0, 0.0)
        ncol[a] = ncol[a] + sel
        zsum = zsum + sel * (e1[a] * e2[b])
    inv_z = 1.0 / zsum

    def dense(i, carry):
        rows = _key_rows(i)
        key = i.astype(F32)
        n1 = zero
        rk2 = jnp.full((SUBLANES, LANES), float(PEER_TOPK), F32)
        for k in range(PEER_TOPK):
            n1 = jnp.where(si1[k] == key, ncol[k], n1)
            rk2 = jnp.where(si2[k] == key, float(k), rk2)
        n1_s[rows, :] = n1
        rk2_s[rows, :] = rk2
        s1_s[rows, :] = jnp.exp(s1_s[rows, :] - sc1[0]) * inv_z
        s2_s[rows, :] = jnp.exp(s2_s[rows, :] - sc2[0])
        return carry

    lax.fori_loop(0, PEER_NKEYS, dense, 0)
    _to_token_lanes(rk2_s, r2_ref)
    _to_token_lanes(s2_s, e2_ref)
    _to_token_lanes(n1_s, n1_ref)
    _to_token_lanes(s1_s, c1_ref)


def _route(hn, w_q, keys):
    T = hn.shape[0]
    tm = ROUTE_TM
    out_sds = jax.ShapeDtypeStruct((PEER_HEADS, T // LANES, PEER_NKEYS, LANES), F32)
    out_spec = pl.BlockSpec((None, SUBLANES, PEER_NKEYS, LANES), lambda i, h: (h, i, 0, 0))
    km = pltpu.VMEM((PEER_NKEYS * SUBLANES, LANES), F32)
    return pl.pallas_call(
        _route_body,
        out_shape=(out_sds,) * 4,
        grid=(T // tm, PEER_HEADS),
        in_specs=[pl.BlockSpec((tm, D_MODEL), lambda i, h: (i, 0)),
                  pl.BlockSpec((D_MODEL, 2 * PEER_HALF), lambda i, h: (0, h)),
                  pl.BlockSpec((None, 2, PEER_NKEYS, PEER_HALF), lambda i, h: (h, 0, 0, 0))],
        out_specs=(out_spec,) * 4,
        scratch_shapes=[km] * 4,
        compiler_params=_cparams(2),
        name="route",
    )(hn, w_q, keys)


PEER_TE = 512
INV_SQRT2 = 0.7071067811865476


PEER_QROWS = 32


def _peer_tile(a_prev, a_next, jprev, live, hn_ref, u_ref, vt_ref, r2_ref, e2_ref, n1_ref, c1_ref, o_ref, g_scr):
    n_il = PEER_TE // PEER_NKEYS
    reps = PEER_QROWS // SUBLANES
    n_cb = g_scr.shape[1] // LANES
    ktile = 2 * PEER_NKEYS

    def up_chunk(c, n):
        rs = slice(c * PEER_TE // n, (c + 1) * PEER_TE // n)
        a_next[rs, :] = lax.dot_general(u_ref[rs, :], hn_ref[...], NT, preferred_element_type=F32)

    def gate(ils, cb):
        lanes = slice(cb * LANES, (cb + 1) * LANES)
        for qd in range(PEER_NKEYS // PEER_QROWS):
            rows = slice(qd * PEER_QROWS, (qd + 1) * PEER_QROWS)
            w = {il: None for il in ils}
            for h in range(PEER_HEADS):
                r2 = r2_ref[h, cb, rows, :]
                e2 = e2_ref[h, cb, rows, :]
                for il in ils:
                    key = pl.ds(jprev * n_il + il, SUBLANES, stride=0)
                    n1 = pltpu.repeat(n1_ref[h, cb, key, :], reps, axis=0)
                    c1 = pltpu.repeat(c1_ref[h, cb, key, :], reps, axis=0)
                    t = jnp.where(r2 < n1, e2, 0.0) * c1
                    w[il] = t if w[il] is None else w[il] + t
            for il in ils:
                er = slice(il * PEER_NKEYS + qd * PEER_QROWS, il * PEER_NKEYS + (qd + 1) * PEER_QROWS)
                a = a_prev[er, lanes]
                act = 0.5 * a * (1.0 + lax.erf(a * INV_SQRT2))
                g_scr[er, lanes] = (act * (w[il] * live)).astype(BF16)

    def down(kt):
        ks = slice(kt * ktile, (kt + 1) * ktile)
        return jnp.dot(vt_ref[:, ks], g_scr[ks, :], preferred_element_type=F32)

    up_chunk(0, 1)
    for kt in range(PEER_TE // ktile):
        for cb in range(n_cb):
            gate((2 * kt, 2 * kt + 1), cb)
    o_ref[...] += jnp.dot(vt_ref[...], g_scr[...], preferred_element_type=F32)


def _peer_body(hn_ref, u_ref, vt_ref, r2_ref, e2_ref, n1_ref, c1_ref, o_ref, a0, a1, g_scr):
    i = pl.program_id(0)
    j = pl.program_id(1)

    @pl.when(jnp.logical_and(i == 0, j == 0))
    def _():
        a1[...] = jnp.zeros_like(a1)

    @pl.when(j == 0)
    def _():
        o_ref[...] = jnp.zeros_like(o_ref)

    live = jnp.where(j > 0, 1.0, 0.0).astype(F32)
    jprev = jnp.maximum(j - 1, 0)
    tile = functools.partial(_peer_tile, jprev=jprev, live=live, hn_ref=hn_ref, u_ref=u_ref, vt_ref=vt_ref,
                             r2_ref=r2_ref, e2_ref=e2_ref, n1_ref=n1_ref, c1_ref=c1_ref, o_ref=o_ref, g_scr=g_scr)

    @pl.when(j % 2 == 0)
    def _():
        tile(a1, a0)

    @pl.when(j % 2 == 1)
    def _():
        tile(a0, a1)


def _peer(hn, u, vt, r2, e2, n1, c1, tm=512):
    T = hn.shape[0]
    te = PEER_TE
    nj = PEER_EXPERTS // te
    rspec = pl.BlockSpec((PEER_HEADS, tm // LANES, PEER_NKEYS, LANES), lambda i, j: (0, i, 0, 0))
    return pl.pallas_call(
        _peer_body,
        out_shape=jax.ShapeDtypeStruct((D_MODEL, T), F32),
        grid=(T // tm, nj + 1),
        in_specs=[pl.BlockSpec((tm, D_MODEL), lambda i, j: (i, 0)),
                  pl.BlockSpec((te, D_MODEL), lambda i, j: (jnp.minimum(j, nj - 1), 0)),
                  pl.BlockSpec((D_MODEL, te), lambda i, j: (0, jnp.maximum(j - 1, 0))),
                  rspec, rspec, rspec, rspec],
        out_specs=pl.BlockSpec((D_MODEL, tm), lambda i, j: (0, i)),
        scratch_shapes=[pltpu.VMEM((te, tm), F32), pltpu.VMEM((te, tm), F32), pltpu.VMEM((te, tm), BF16)],
        compiler_params=_cparams(2),
        name="peer",
    )(hn, u, vt, r2, e2, n1, c1)


def _final_body(h_ref, pt_ref, g_ref, o_ref):
    o_ref[...] = _rms(h_ref[...] + pt_ref[...].T, g_ref[...])


def _final(h, peer_t, g, tm=512):
    T = h.shape[0]
    return pl.pallas_call(
        _final_body,
        out_shape=jax.ShapeDtypeStruct((T, D_MODEL), F32),
        grid=(T // tm,),
        in_specs=[pl.BlockSpec((tm, D_MODEL), lambda i: (i, 0)),
                  pl.BlockSpec((D_MODEL, tm), lambda i: (0, i)),
                  pl.BlockSpec((1, D_MODEL), lambda i: (0, 0))],
        out_specs=pl.BlockSpec((tm, D_MODEL), lambda i: (i, 0)),
        compiler_params=_cparams(1),
        name="final",
    )(h, peer_t, g)


def _layer(x, norm1_g, w_in, up_f, bias_f, up_b, bias_b, gla_norm_g, att_norm_g, w_out, norm2_g,
           w_q, sub_keys, peer_u, peer_v):
    B, S, D = x.shape
    T = B * S
    x2 = x.reshape(T, D)
    gd0 = 2 * GLA_QK + 2 * GLA_V
    w_main = jnp.concatenate([w_in[:, :gd0], w_in[:, gd0 + GD_W:]], axis=1).astype(BF16)
    w_gd = w_in[:, gd0:gd0 + GD_W].astype(BF16)
    main, gd = _inproj(x2, norm1_g.reshape(1, D), w_main, w_gd)
    main3 = main.reshape(B, S, MAIN_W)
    o_f, o_b = _gla(main3, gd.reshape(B, S, GD_W), up_f, bias_f.reshape(1, GLA_QK),
                    up_b, bias_b.reshape(1, GLA_QK))
    slopes = jnp.asarray((2.0 ** (-8.0 * np.arange(1, ATT_HEADS + 1) / ATT_HEADS)).astype(np.float32))
    att = _attn(main3, slopes)
    h, hn = _mix(o_f.reshape(T, GLA_V), o_b.reshape(T, GLA_V), main, att.reshape(T, ATT_W), x2,
                 gla_norm_g.reshape(1, GLA_DV), att_norm_g.reshape(1, ATT_W), w_out.astype(BF16),
                 norm2_g.reshape(1, D))
    r2, e2, n1, c1 = _route(hn, w_q.astype(BF16), sub_keys.astype(BF16))
    peer_t = _peer(hn, peer_u.astype(BF16), peer_v.astype(BF16).T, r2, e2, n1, c1)
    return h, peer_t


def kernel(x, norm1_g, w_in, gla_gate_up_f, gla_gate_bias_f, gla_gate_up_b, gla_gate_bias_b, gla_norm_g,
           att_norm_g, w_out, norm2_g, peer_w_q, peer_sub_keys, peer_u, peer_v, final_norm_g):
    B, S, D = x.shape
    assert norm1_g.shape[0] == 1, "single trunk layer: the final norm is fused with the layer's last residual"
    h, peer_t = _layer(x, norm1_g[0], w_in[0], gla_gate_up_f[0], gla_gate_bias_f[0], gla_gate_up_b[0],
                       gla_gate_bias_b[0], gla_norm_g[0], att_norm_g[0], w_out[0], norm2_g[0],
                       peer_w_q[0], peer_sub_keys[0], peer_u[0], peer_v[0])
    return _final(h, peer_t, final_norm_g.reshape(1, D)).reshape(B, S, D)
```

```python
import functools

import numpy as np
import jax
import jax.numpy as jnp
from jax import lax
from jax.experimental import pallas as pl
from jax.experimental.pallas import tpu as pltpu

F32 = jnp.float32
BF16 = jnp.bfloat16
I32 = jnp.int32

D_MODEL = 2048
GLA_HEADS = 4
GLA_DK = 128
GLA_DV = 256
GLA_RANK = 16
GLA_TAU = 16.0
GLA_CHUNK = 64
GLA_QK = GLA_HEADS * GLA_DK
GLA_V = GLA_HEADS * GLA_DV
ATT_HEADS = 16
ATT_HD = 64
ATT_W = ATT_HEADS * ATT_HD
ATT_DILATIONS = (1, 4, 16)
ATT_RADIUS = 64
PEER_NKEYS = 128
PEER_HEADS = 8
PEER_TOPK = 16
PEER_EXPERTS = PEER_NKEYS * PEER_NKEYS
PEER_HALF = 128
EPS = 1e-6
NEG = -1e30
MAIN_W = 2 * GLA_QK + 2 * GLA_V + 3 * ATT_W
GD_W = 2 * GLA_RANK

LANES = 128
SUBLANES = 8
VMEM_LIMIT = 56 * 1024 * 1024

NT = (((1,), (1,)), ((), ()))
TN = (((0,), (0,)), ((), ()))


def _cparams(n_axes, flags=None):
    return pltpu.CompilerParams(dimension_semantics=("arbitrary",) * n_axes,
                                vmem_limit_bytes=VMEM_LIMIT, flags=flags)


def _rms(x, g):
    ms = jnp.mean(x * x, axis=-1, keepdims=True)
    return x * lax.rsqrt(ms + EPS) * g


def _inproj_body(x_ref, g_ref, w_ref, wgd_ref, o_ref, gd_ref, xn_ref):
    @pl.when(pl.program_id(1) == 0)
    def _():
        xn = _rms(x_ref[...], g_ref[...]).astype(BF16)
        xn_ref[...] = xn
        gd_ref[...] = jnp.dot(xn, wgd_ref[...], preferred_element_type=F32)

    o_ref[...] = jnp.dot(xn_ref[...], w_ref[...], preferred_element_type=F32)


def _inproj(x2, g, w_main, w_gd, tm=1024, tn=1024):
    T = x2.shape[0]
    return pl.pallas_call(
        _inproj_body,
        out_shape=(jax.ShapeDtypeStruct((T, MAIN_W), F32), jax.ShapeDtypeStruct((T, GD_W), F32)),
        grid=(T // tm, MAIN_W // tn),
        in_specs=[pl.BlockSpec((tm, D_MODEL), lambda i, j: (i, 0)),
                  pl.BlockSpec((1, D_MODEL), lambda i, j: (0, 0)),
                  pl.BlockSpec((D_MODEL, tn), lambda i, j: (0, j)),
                  pl.BlockSpec((D_MODEL, GD_W), lambda i, j: (0, 0))],
        out_specs=(pl.BlockSpec((tm, tn), lambda i, j: (i, j)),
                   pl.BlockSpec((tm, GD_W), lambda i, j: (i, 0))),
        scratch_shapes=[pltpu.VMEM((tm, D_MODEL), BF16)],
        compiler_params=_cparams(2),
        name="inproj",
    )(x2, g, w_main, w_gd)


def _log_gate(gd, up_ref, bias_ref):
    z = jnp.dot(gd.astype(BF16), up_ref[...].astype(BF16), preferred_element_type=F32) + bias_ref[...]
    return (jnp.minimum(z, 0.0) - jnp.log1p(jnp.exp(-jnp.abs(z)))) * (1.0 / GLA_TAU)


def _gla_chunk(q_ref, k_ref, v_ref, la_ref, st_ref, o_ref, r0, tri_mask, tot_row):
    C = GLA_CHUNK
    sl = pl.ds(pl.multiple_of(r0, C), C)
    la = la_ref[sl, :]
    hi = la.astype(BF16)
    lo = (la - hi.astype(F32)).astype(BF16)
    tri = jnp.where(tri_mask, 1.0, 0.0).astype(BF16)
    b = (jnp.dot(tri, hi, preferred_element_type=F32) + jnp.dot(tri, lo, preferred_element_type=F32))
    btot = b[tot_row:tot_row + 1, :]
    q = q_ref[sl, :]
    k = k_ref[sl, :]
    v = v_ref[sl, :].astype(BF16)
    qd = (q * (jnp.exp(b) * (GLA_DK ** -0.5))).astype(BF16)
    kd = (k * jnp.exp(-b)).astype(BF16)
    kr = (k * jnp.exp(btot - b)).astype(BF16)
    att = lax.dot_general(qd, kd, NT, preferred_element_type=F32)
    att = jnp.where(tri_mask, att, 0.0).astype(BF16)
    st = st_ref[...]
    o = (jnp.dot(att, v, preferred_element_type=F32)
         + lax.dot_general(qd, st.astype(BF16), NT, preferred_element_type=F32))
    o_ref[sl, :] = o
    st_ref[...] = jnp.exp(btot) * st + lax.dot_general(v, kr, TN, preferred_element_type=F32)


def _gla_body(qf, kf, vf, gdf, qb, kb, vb, gdb, upf, bsf, upb, bsb, of_ref, ob_ref,
              stf, stb, laf, lab, *, nchunk):
    @pl.when(pl.program_id(2) == 0)
    def _():
        stf[...] = jnp.zeros_like(stf)
        stb[...] = jnp.zeros_like(stb)

    laf[...] = _log_gate(gdf[:, 0:GLA_RANK], upf, bsf)
    lab[...] = _log_gate(gdb[:, GLA_RANK:2 * GLA_RANK], upb, bsb)
    C = GLA_CHUNK
    row = lax.broadcasted_iota(I32, (C, C), 0)
    col = lax.broadcasted_iota(I32, (C, C), 1)
    lower = row >= col
    upper = col >= row

    def step(c, carry):
        _gla_chunk(qf, kf, vf, laf, stf, of_ref, c * C, lower, C - 1)
        _gla_chunk(qb, kb, vb, lab, stb, ob_ref, (nchunk - 1 - c) * C, upper, 0)
        return carry

    lax.fori_loop(0, nchunk, step, 0)


def _gla(main3, gd3, up_f, bias_f, up_b, bias_b, lb=512):
    B, S, _ = main3.shape
    nb = S // lb
    kq = GLA_QK // GLA_DK
    kv = (2 * GLA_QK) // GLA_DV
    fwd = lambda b, h, i: (b, i, h)
    blk = lambda c0: (lambda b, h, i: (b, i, c0 + h))
    rblk = lambda c0: (lambda b, h, i: (b, nb - 1 - i, c0 + h))
    sq = lambda m: pl.BlockSpec((None, lb, GLA_DK), m)
    sv = lambda m: pl.BlockSpec((None, lb, GLA_DV), m)
    sg = lambda m: pl.BlockSpec((None, lb, GD_W), m)
    up_spec = pl.BlockSpec((GLA_RANK, GLA_DK), lambda b, h, i: (0, h))
    bias_spec = pl.BlockSpec((1, GLA_DK), lambda b, h, i: (0, h))
    out_sds = jax.ShapeDtypeStruct((B, S, GLA_V), F32)
    return pl.pallas_call(
        functools.partial(_gla_body, nchunk=lb // GLA_CHUNK),
        out_shape=(out_sds, out_sds),
        grid=(B, GLA_HEADS, nb),
        in_specs=[sq(blk(0)), sq(blk(kq)), sv(blk(kv)), sg(lambda b, h, i: (b, i, 0)),
                  sq(rblk(0)), sq(rblk(kq)), sv(rblk(kv)), sg(lambda b, h, i: (b, nb - 1 - i, 0)),
                  up_spec, bias_spec, up_spec, bias_spec],
        out_specs=(pl.BlockSpec((None, lb, GLA_DV), fwd),
                   pl.BlockSpec((None, lb, GLA_DV), lambda b, h, i: (b, nb - 1 - i, h))),
        scratch_shapes=[pltpu.VMEM((GLA_DV, GLA_DK), F32), pltpu.VMEM((GLA_DV, GLA_DK), F32),
                        pltpu.VMEM((lb, GLA_DK), F32), pltpu.VMEM((lb, GLA_DK), F32)],
        compiler_params=_cparams(3),
        name="gla",
    )(main3, main3, main3, gd3, main3, main3, main3, gd3, up_f, bias_f, up_b, bias_b)


ATT_TQ = 128
ATT_WIN = 256


ATT_GROUP = 4


ATT_FACTOR = 4
assert all(b == a * ATT_FACTOR for a, b in zip(ATT_DILATIONS[:-1], ATT_DILATIONS[1:])) and ATT_DILATIONS[0] == 1


def _split_classes(src, dst, d, n):
    f = ATT_FACTOR
    sub = n // (d * f)
    for r in range(d):
        for a in range(f):
            dst[pl.ds((d * a + r) * sub, sub), :] = src[pl.ds(r * (n // d) + a, sub, stride=f), :]


def _merge_classes(src, dst, d, n):
    f = ATT_FACTOR
    sub = n // (d * f)
    for r in range(d):
        for a in range(f):
            dst[pl.ds(r * (n // d) + a, sub, stride=f), :] = src[pl.ds((d * a + r) * sub, sub), :]


def _attn_body(slopes_ref, q_ref, k_ref, v_ref, o_ref, *scr, seq):
    pair = pl.program_id(1)
    lane = lax.broadcasted_iota(I32, (1, LANES), 1)
    head0 = lane < ATT_HD
    ii = lax.broadcasted_iota(I32, (ATT_TQ, ATT_WIN), 0)
    jj = lax.broadcasted_iota(I32, (ATT_TQ, ATT_WIN), 1)
    slope = (slopes_ref[2 * pair], slopes_ref[2 * pair + 1])
    npat = len(ATT_DILATIONS)
    copies = [(q_ref, k_ref, v_ref)] + [scr[3 * i:3 * i + 3] for i in range(npat - 1)]
    stats = [scr[3 * (npat - 1):3 * npat], scr[3 * npat:3 * (npat + 1)]]
    bias_s = scr[3 * (npat + 1)]
    for pi in range(1, npat):
        for x in range(3):
            _split_classes(copies[pi - 1][x], copies[pi][x], ATT_DILATIONS[pi - 1], seq)

    for step, pi in enumerate(reversed(range(npat))):
        d = ATT_DILATIONS[pi]
        L = seq // d
        nblk = L // ATT_TQ
        first = step == 0
        q_ref, k_ref, v_ref = copies[pi]
        if not first:
            for x in range(3):
                _merge_classes(stats[(step - 1) % 2][x], stats[step % 2][x], d, seq)
        m_s, l_s, acc_s = stats[step % 2]
        for case in range(3):
            ad = jnp.abs(jj - ii - case * ATT_RADIUS)
            for hh in range(2):
                bias_s[case, hh] = jnp.where(ad <= ATT_RADIUS, -slope[hh] * (ad * d).astype(F32), NEG)

        def group(gi, carry, L=L, nblk=nblk, first=first, q_ref=q_ref, k_ref=k_ref, v_ref=v_ref,
                  m_s=m_s, l_s=l_s, acc_s=acc_s):
            loaded = []
            for u in range(ATT_GROUP):
                t = gi * ATT_GROUP + u
                base = (t // nblk) * L
                q0 = (t % nblk) * ATT_TQ
                start = jnp.clip(q0 - ATT_RADIUS, 0, L - ATT_WIN)
                case = (q0 - start) // ATT_RADIUS
                qsl = pl.ds(pl.multiple_of(base + q0, ATT_TQ), ATT_TQ)
                ksl = pl.ds(pl.multiple_of(base + start, ATT_RADIUS), ATT_WIN)
                q = q_ref[qsl, :] * (ATT_HD ** -0.5)
                k = k_ref[ksl, :].astype(BF16)
                v = v_ref[ksl, :].astype(BF16)
                run = None if first else (m_s[qsl, :], l_s[qsl, :], acc_s[qsl, :])
                loaded.append((qsl, case, q, k, v, run))
            results = []
            for qsl, case, q, k, v, run in loaded:
                m_new, lsum, pv = [], [], []
                for hh in range(2):
                    hm = head0 if hh == 0 else jnp.logical_not(head0)
                    qh = jnp.where(hm, q, 0.0).astype(BF16)
                    s = lax.dot_general(qh, k, NT, preferred_element_type=F32) + bias_s[case, hh]
                    mh = jnp.max(s, axis=-1, keepdims=True)
                    p = jnp.exp(s - mh)
                    m_new.append(mh)
                    lsum.append(jnp.sum(p, axis=-1, keepdims=True))
                    pv.append(jnp.dot(p.astype(BF16), v, preferred_element_type=F32))
                m_full = jnp.where(head0, m_new[0], m_new[1])
                l_blk = jnp.where(head0, lsum[0], lsum[1])
                pv_blk = jnp.where(head0, pv[0], pv[1])
                if not first:
                    m_blk = m_full
                    m_full = jnp.maximum(run[0], m_blk)
                    a_run = jnp.exp(run[0] - m_full)
                    a_blk = jnp.exp(m_blk - m_full)
                    l_blk = a_run * run[1] + a_blk * l_blk
                    pv_blk = a_run * run[2] + a_blk * pv_blk
                results.append((qsl, m_full, l_blk, pv_blk))
            for qsl, m_full, l_blk, pv_blk in results:
                m_s[qsl, :] = m_full
                l_s[qsl, :] = l_blk
                acc_s[qsl, :] = pv_blk
            return carry

        lax.fori_loop(0, d * nblk // ATT_GROUP, group, 0)

    o_ref[...] = acc_s[...] / l_s[...]


def _attn(main3, slopes):
    B, S, _ = main3.shape
    c0 = (2 * GLA_QK + 2 * GLA_V) // LANES
    npair = ATT_HEADS // 2
    spec = lambda off: pl.BlockSpec((None, S, LANES), lambda b, p: (b, 0, c0 + off + p))
    return pl.pallas_call(
        functools.partial(_attn_body, seq=S),
        out_shape=jax.ShapeDtypeStruct((B, S, ATT_W), F32),
        grid=(B, npair),
        in_specs=[pl.BlockSpec(memory_space=pltpu.SMEM), spec(0), spec(npair), spec(2 * npair)],
        out_specs=pl.BlockSpec((None, S, LANES), lambda b, p: (b, 0, p)),
        scratch_shapes=([pltpu.VMEM((S, LANES), F32)] * (3 * (len(ATT_DILATIONS) + 1))
                        + [pltpu.VMEM((3, 2, ATT_TQ, ATT_WIN), F32)]),
        compiler_params=_cparams(2),
        name="attn",
    )(slopes, main3, main3, main3)


def _mix_body(of_ref, ob_ref, gr_ref, att_ref, x_ref, gg_ref, ag_ref, wo_ref, n2_ref, h_ref, hn_ref):
    o = of_ref[...] + ob_ref[...]
    parts = [_rms(o[:, h * GLA_DV:(h + 1) * GLA_DV], gg_ref[...]) for h in range(GLA_HEADS)]
    on = jnp.concatenate(parts, axis=-1)
    gr = gr_ref[...]
    gla = on * (gr / (1.0 + jnp.exp(-gr)))
    an = _rms(att_ref[...], ag_ref[...])
    y = (jnp.dot(gla.astype(BF16), wo_ref[0:GLA_V, :], preferred_element_type=F32)
         + jnp.dot(an.astype(BF16), wo_ref[GLA_V:GLA_V + ATT_W, :], preferred_element_type=F32))
    h = x_ref[...] + y
    h_ref[...] = h
    hn_ref[...] = _rms(h, n2_ref[...]).astype(BF16)


def _mix(o_f, o_b, main2, att2, x2, gg, ag, w_out, n2g, tm=512):
    T = x2.shape[0]
    gr_blk = (2 * GLA_QK + GLA_V) // GLA_V
    row = lambda i: (i, 0)
    const = lambda i: (0, 0)
    return pl.pallas_call(
        _mix_body,
        out_shape=(jax.ShapeDtypeStruct((T, D_MODEL), F32), jax.ShapeDtypeStruct((T, D_MODEL), BF16)),
        grid=(T // tm,),
        in_specs=[pl.BlockSpec((tm, GLA_V), row), pl.BlockSpec((tm, GLA_V), row),
                  pl.BlockSpec((tm, GLA_V), lambda i: (i, gr_blk)),
                  pl.BlockSpec((tm, ATT_W), row), pl.BlockSpec((tm, D_MODEL), row),
                  pl.BlockSpec((1, GLA_DV), const), pl.BlockSpec((1, ATT_W), const),
                  pl.BlockSpec((D_MODEL, D_MODEL), const), pl.BlockSpec((1, D_MODEL), const)],
        out_specs=(pl.BlockSpec((tm, D_MODEL), row), pl.BlockSpec((tm, D_MODEL), row)),
        compiler_params=_cparams(1),
        name="mix",
    )(o_f, o_b, main2, att2, x2, gg, ag, w_out, n2g)


ROUTE_TM = SUBLANES * LANES
_CELLS = [(a, b) for a in range(PEER_TOPK) for b in range(PEER_TOPK) if (a + 1) * (b + 1) <= PEER_TOPK]


def _key_rows(i):
    return pl.ds(i * SUBLANES, SUBLANES)


def _before(a, b):
    (va, ia), (vb, ib) = a, b
    if isinstance(ia, float) and isinstance(ib, float):
        return (va >= vb) if ia < ib else (va > vb)
    return jnp.logical_or(va > vb, jnp.logical_and(va == vb, ia < ib))


def _pick(c, a, b):
    return tuple(jnp.where(c, x, y) for x, y in zip(a, b))


def _ordered_pair(a, b):
    c = _before(a, b)
    return _pick(c, a, b), _pick(c, b, a)


def _bitonic_merge(xs):
    n = len(xs)
    if n == 1:
        return list(xs)
    h = n // 2
    xs = list(xs)
    for k in range(h):
        xs[k], xs[k + h] = _ordered_pair(xs[k], xs[k + h])
    return _bitonic_merge(xs[:h]) + _bitonic_merge(xs[h:])


def _bitonic_sort(xs):
    n = len(xs)
    if n == 1:
        return list(xs)
    return _bitonic_merge(_bitonic_sort(xs[:n // 2]) + _bitonic_sort(xs[n // 2:])[::-1])


def _top_keys(s_ref):
    k = PEER_TOPK
    best = None
    for g in range(PEER_NKEYS // k):
        grp = _bitonic_sort([(s_ref[_key_rows(g * k + a), :], float(g * k + a)) for a in range(k)])
        grp = [(v, jnp.full((SUBLANES, LANES), i, F32) if isinstance(i, float) else i) for v, i in grp]
        if best is None:
            best = grp
        else:
            best = _bitonic_merge([_pick(_before(best[j], grp[k - 1 - j]), best[j], grp[k - 1 - j])
                                   for j in range(k)])
    return [v for v, _ in best], [i for _, i in best]


def _to_token_lanes(src_ref, dst_ref):
    for g in range(SUBLANES):
        dst_ref[g] = src_ref[pl.ds(g, PEER_NKEYS, stride=SUBLANES), :].astype(dst_ref.dtype)


def _route_body(hn_ref, wq_ref, keys_ref, r2_ref, e2_ref, n1_ref, c1_ref, s1_s, s2_s, rk2_s, n1_s):
    q = jnp.dot(hn_ref[...], wq_ref[...], preferred_element_type=F32)
    for c, s_s in enumerate((s1_s, s2_s)):
        qc = q[:, c * PEER_HALF:(c + 1) * PEER_HALF].astype(BF16)
        st = lax.dot_general(keys_ref[c], qc, NT, preferred_element_type=F32)
        for g in range(SUBLANES):
            s_s[pl.ds(g, PEER_NKEYS, stride=SUBLANES), :] = st[:, g * LANES:(g + 1) * LANES]
    sc1, si1 = _top_keys(s1_s)
    sc2, si2 = _top_keys(s2_s)
    e1 = [jnp.exp(sc1[k] - sc1[0]) for k in range(PEER_TOPK)]
    e2 = [jnp.exp(sc2[k] - sc2[0]) for k in range(PEER_TOPK)]

    cand = [sc1[a] + sc2[b] for (a, b) in _CELLS]
    nc = len(_CELLS)
    before = [jnp.zeros((SUBLANES, LANES), F32) for _ in range(nc)]
    for x in range(nc):
        for y in range(x + 1, nc):
            ax, bx = _CELLS[x]
            ay, by = _CELLS[y]
            if ax <= ay and bx <= by:
                before[y] = before[y] + 1.0
                continue
            bt = jnp.where(cand[x] >= cand[y], 1.0, 0.0)
            before[y] = before[y] + bt
            before[x] = before[x] + (1.0 - bt)
    zero = jnp.zeros((SUBLANES, LANES), F32)
    ncol = [zero] * PEER_TOPK
    zsum = zero
    for x, (a, b) in enumerate(_CELLS):
        sel = jnp.where(before[x] < float(PEER_TOPK), 1.0, 0.0)
        ncol[a] = ncol[a] + sel
        zsum = zsum + sel * (e1[a] * e2[b])
    inv_z = 1.0 / zsum

    def dense(i, carry):
        rows = _key_rows(i)
        key = lax.convert_element_type(i, F32)
        n1 = zero
        rk2 = jnp.full((SUBLANES, LANES), float(PEER_TOPK), F32)
        for k in range(PEER_TOPK):
            n1 = jnp.where(si1[k] == key, ncol[k], n1)
            rk2 = jnp.where(si2[k] == key, float(k), rk2)
        n1_s[rows, :] = n1
        rk2_s[rows, :] = rk2
        s1_s[rows, :] = jnp.exp(s1_s[rows, :] - sc1[0]) * inv_z
        s2_s[rows, :] = jnp.exp(s2_s[rows, :] - sc2[0])
        return carry

    lax.fori_loop(0, PEER_NKEYS, dense, 0)
    _to_token_lanes(rk2_s, r2_ref)
    _to_token_lanes(s2_s, e2_ref)
    _to_token_lanes(n1_s, n1_ref)
    _to_token_lanes(s1_s, c1_ref)


def _route(hn, w_q, keys):
    T = hn.shape[0]
    tm = ROUTE_TM
    out_sds = lambda dt: jax.ShapeDtypeStruct((PEER_HEADS, T // LANES, PEER_NKEYS, LANES), dt)
    out_spec = pl.BlockSpec((None, SUBLANES, PEER_NKEYS, LANES), lambda i, h: (h, i, 0, 0))
    km = pltpu.VMEM((PEER_NKEYS * SUBLANES, LANES), F32)
    return pl.pallas_call(
        _route_body,
        out_shape=(out_sds(F32),) * 4,
        grid=(T // tm, PEER_HEADS),
        in_specs=[pl.BlockSpec((tm, D_MODEL), lambda i, h: (i, 0)),
                  pl.BlockSpec((D_MODEL, 2 * PEER_HALF), lambda i, h: (0, h)),
                  pl.BlockSpec((None, 2, PEER_NKEYS, PEER_HALF), lambda i, h: (h, 0, 0, 0))],
        out_specs=(out_spec,) * 4,
        scratch_shapes=[km] * 4,
        compiler_params=_cparams(2),
        name="route",
    )(hn, w_q, keys)


PEER_TE = 512
INV_SQRT2 = 0.7071067811865476


PEER_QROWS = 32


def _peer_tile(a_prev, a_next, jprev, live, hn_ref, u_ref, vt_ref, r2_ref, e2_ref, n1_ref, c1_ref, o_ref, g_scr):
    n_il = PEER_TE // PEER_NKEYS
    reps = PEER_QROWS // SUBLANES
    n_cb = g_scr.shape[1] // LANES

    def gate(ils, cb):
        lanes = slice(cb * LANES, (cb + 1) * LANES)
        for qd in range(PEER_NKEYS // PEER_QROWS):
            rows = slice(qd * PEER_QROWS, (qd + 1) * PEER_QROWS)
            w = {il: None for il in ils}
            for h in range(PEER_HEADS):
                r2 = r2_ref[h, cb, rows, :]
                e2 = e2_ref[h, cb, rows, :]
                for il in ils:
                    key = pl.ds(jprev * n_il + il, SUBLANES, stride=0)
                    n1 = pltpu.repeat(n1_ref[h, cb, key, :], reps, axis=0)
                    c1 = pltpu.repeat(c1_ref[h, cb, key, :], reps, axis=0)
                    t = jnp.where(r2 < n1, e2, 0.0) * c1
                    w[il] = t if w[il] is None else w[il] + t
            for il in ils:
                er = slice(il * PEER_NKEYS + qd * PEER_QROWS, il * PEER_NKEYS + (qd + 1) * PEER_QROWS)
                a = a_prev[er, lanes]
                act = 0.5 * a * (1.0 + lax.erf(a * INV_SQRT2))
                g_scr[er, lanes] = (act * (w[il] * live)).astype(BF16)

    a_next[...] = lax.dot_general(u_ref[...], hn_ref[...], NT, preferred_element_type=F32)
    for half in range(n_il // 2):
        for cb in range(n_cb):
            gate((2 * half, 2 * half + 1), cb)
    o_ref[...] += jnp.dot(vt_ref[...], g_scr[...], preferred_element_type=F32)


def _peer_body(hn_ref, u_ref, vt_ref, r2_ref, e2_ref, n1_ref, c1_ref, o_ref, a0, a1, g_scr):
    i = pl.program_id(0)
    j = pl.program_id(1)

    @pl.when(jnp.logical_and(i == 0, j == 0))
    def _():
        a1[...] = jnp.zeros_like(a1)

    @pl.when(j == 0)
    def _():
        o_ref[...] = jnp.zeros_like(o_ref)

    live = jnp.where(j > 0, 1.0, 0.0).astype(F32)
    jprev = jnp.maximum(j - 1, 0)
    tile = functools.partial(_peer_tile, jprev=jprev, live=live, hn_ref=hn_ref, u_ref=u_ref, vt_ref=vt_ref,
                             r2_ref=r2_ref, e2_ref=e2_ref, n1_ref=n1_ref, c1_ref=c1_ref, o_ref=o_ref, g_scr=g_scr)

    @pl.when(j % 2 == 0)
    def _():
        tile(a1, a0)

    @pl.when(j % 2 == 1)
    def _():
        tile(a0, a1)


def _peer(hn, u, vt, r2, e2, n1, c1, tm=512):
    T = hn.shape[0]
    te = PEER_TE
    nj = PEER_EXPERTS // te
    rspec = pl.BlockSpec((PEER_HEADS, tm // LANES, PEER_NKEYS, LANES), lambda i, j: (0, i, 0, 0))
    return pl.pallas_call(
        _peer_body,
        out_shape=jax.ShapeDtypeStruct((D_MODEL, T), F32),
        grid=(T // tm, nj + 1),
        in_specs=[pl.BlockSpec((tm, D_MODEL), lambda i, j: (i, 0)),
                  pl.BlockSpec((te, D_MODEL), lambda i, j: (jnp.minimum(j, nj - 1), 0)),
                  pl.BlockSpec((None, D_MODEL, te), lambda i, j: (jnp.maximum(j - 1, 0), 0, 0)),
                  rspec, rspec, rspec, rspec],
        out_specs=pl.BlockSpec((D_MODEL, tm), lambda i, j: (0, i)),
        scratch_shapes=[pltpu.VMEM((te, tm), F32), pltpu.VMEM((te, tm), F32), pltpu.VMEM((te, tm), BF16)],
        compiler_params=_cparams(2),
        name="peer",
    )(hn, u, vt, r2, e2, n1, c1)


def _final_body(h_ref, pt_ref, g_ref, o_ref):
    o_ref[...] = _rms(h_ref[...] + pt_ref[...].T, g_ref[...])


def _final(h, peer_t, g, tm=512):
    T = h.shape[0]
    return pl.pallas_call(
        _final_body,
        out_shape=jax.ShapeDtypeStruct((T, D_MODEL), F32),
        grid=(T // tm,),
        in_specs=[pl.BlockSpec((tm, D_MODEL), lambda i: (i, 0)),
                  pl.BlockSpec((D_MODEL, tm), lambda i: (0, i)),
                  pl.BlockSpec((1, D_MODEL), lambda i: (0, 0))],
        out_specs=pl.BlockSpec((tm, D_MODEL), lambda i: (i, 0)),
        compiler_params=_cparams(1),
        name="final",
    )(h, peer_t, g)


def _layer(x, norm1_g, w_in, up_f, bias_f, up_b, bias_b, gla_norm_g, att_norm_g, w_out, norm2_g,
           w_q, sub_keys, peer_u, peer_v):
    B, S, D = x.shape
    T = B * S
    x2 = x.reshape(T, D)
    gd0 = 2 * GLA_QK + 2 * GLA_V
    w_main = jnp.concatenate([w_in[:, :gd0], w_in[:, gd0 + GD_W:]], axis=1).astype(BF16)
    w_gd = w_in[:, gd0:gd0 + GD_W].astype(BF16)
    main, gd = _inproj(x2, norm1_g.reshape(1, D), w_main, w_gd)
    main3 = main.reshape(B, S, MAIN_W)
    o_f, o_b = _gla(main3, gd.reshape(B, S, GD_W), up_f, bias_f.reshape(1, GLA_QK),
                    up_b, bias_b.reshape(1, GLA_QK))
    slopes = jnp.asarray((2.0 ** (-8.0 * np.arange(1, ATT_HEADS + 1) / ATT_HEADS)).astype(np.float32))
    att = _attn(main3, slopes)
    h, hn = _mix(o_f.reshape(T, GLA_V), o_b.reshape(T, GLA_V), main, att.reshape(T, ATT_W), x2,
                 gla_norm_g.reshape(1, GLA_DV), att_norm_g.reshape(1, ATT_W), w_out.astype(BF16),
                 norm2_g.reshape(1, D))
    r2, e2, n1, c1 = _route(hn, w_q.astype(BF16), sub_keys.astype(BF16))
    vt = peer_v.astype(BF16).reshape(PEER_EXPERTS // PEER_TE, PEER_TE, D).transpose(0, 2, 1)
    peer_t = _peer(hn, peer_u.astype(BF16), vt, r2, e2, n1, c1)
    return h, peer_t


def kernel(x, norm1_g, w_in, gla_gate_up_f, gla_gate_bias_f, gla_gate_up_b, gla_gate_bias_b, gla_norm_g,
           att_norm_g, w_out, norm2_g, peer_w_q, peer_sub_keys, peer_u, peer_v, final_norm_g):
    B, S, D = x.shape
    assert norm1_g.shape[0] == 1, "single trunk layer: the final norm is fused with the layer's last residual"
    h, peer_t = _layer(x, norm1_g[0], w_in[0], gla_gate_up_f[0], gla_gate_bias_f[0], gla_gate_up_b[0],
                       gla_gate_bias_b[0], gla_norm_g[0], att_norm_g[0], w_out[0], norm2_g[0],
                       peer_w_q[0], peer_sub_keys[0], peer_u[0], peer_v[0])
    return _final(h, peer_t, final_norm_g.reshape(1, D)).reshape(B, S, D)
```

```python
import functools

import numpy as np
import jax
import jax.numpy as jnp
from jax import lax
from jax.experimental import pallas as pl
from jax.experimental.pallas import tpu as pltpu

F32 = jnp.float32
BF16 = jnp.bfloat16
I32 = jnp.int32

D_MODEL = 2048
GLA_HEADS = 4
GLA_DK = 128
GLA_DV = 256
GLA_RANK = 16
GLA_TAU = 16.0
GLA_CHUNK = 64
GLA_QK = GLA_HEADS * GLA_DK
GLA_V = GLA_HEADS * GLA_DV
ATT_HEADS = 16
ATT_HD = 64
ATT_W = ATT_HEADS * ATT_HD
ATT_DILATIONS = (1, 4, 16)
ATT_RADIUS = 64
PEER_NKEYS = 128
PEER_HEADS = 8
PEER_TOPK = 16
PEER_EXPERTS = PEER_NKEYS * PEER_NKEYS
PEER_HALF = 128
EPS = 1e-6
NEG = -1e30
MAIN_W = 2 * GLA_QK + 2 * GLA_V + 3 * ATT_W
GD_W = 2 * GLA_RANK

LANES = 128
SUBLANES = 8
VMEM_LIMIT = 56 * 1024 * 1024

NT = (((1,), (1,)), ((), ()))
TN = (((0,), (0,)), ((), ()))


def _cparams(n_axes, flags=None):
    return pltpu.CompilerParams(dimension_semantics=("arbitrary",) * n_axes,
                                vmem_limit_bytes=VMEM_LIMIT, flags=flags)


def _rms(x, g):
    ms = jnp.mean(x * x, axis=-1, keepdims=True)
    return x * lax.rsqrt(ms + EPS) * g


def _inproj_body(x_ref, g_ref, w_ref, wgd_ref, o_ref, gd_ref, xn_ref):
    @pl.when(pl.program_id(1) == 0)
    def _():
        xn = _rms(x_ref[...], g_ref[...]).astype(BF16)
        xn_ref[...] = xn
        gd_ref[...] = jnp.dot(xn, wgd_ref[...], preferred_element_type=F32)

    o_ref[...] = jnp.dot(xn_ref[...], w_ref[...], preferred_element_type=F32)


def _inproj(x2, g, w_main, w_gd, tm=1024, tn=1024):
    T = x2.shape[0]
    return pl.pallas_call(
        _inproj_body,
        out_shape=(jax.ShapeDtypeStruct((T, MAIN_W), F32), jax.ShapeDtypeStruct((T, GD_W), F32)),
        grid=(T // tm, MAIN_W // tn),
        in_specs=[pl.BlockSpec((tm, D_MODEL), lambda i, j: (i, 0)),
                  pl.BlockSpec((1, D_MODEL), lambda i, j: (0, 0)),
                  pl.BlockSpec((D_MODEL, tn), lambda i, j: (0, j)),
                  pl.BlockSpec((D_MODEL, GD_W), lambda i, j: (0, 0))],
        out_specs=(pl.BlockSpec((tm, tn), lambda i, j: (i, j)),
                   pl.BlockSpec((tm, GD_W), lambda i, j: (i, 0))),
        scratch_shapes=[pltpu.VMEM((tm, D_MODEL), BF16)],
        compiler_params=_cparams(2),
        name="inproj",
    )(x2, g, w_main, w_gd)


def _log_gate(gd, up_ref, bias_ref):
    z = jnp.dot(gd.astype(BF16), up_ref[...].astype(BF16), preferred_element_type=F32) + bias_ref[...]
    return (jnp.minimum(z, 0.0) - jnp.log1p(jnp.exp(-jnp.abs(z)))) * (1.0 / GLA_TAU)


def _gla_chunk(q_ref, k_ref, v_ref, la_ref, st_ref, o_ref, r0, tri_mask, tot_row):
    C = GLA_CHUNK
    sl = pl.ds(pl.multiple_of(r0, C), C)
    la = la_ref[sl, :]
    hi = la.astype(BF16)
    lo = (la - hi.astype(F32)).astype(BF16)
    tri = jnp.where(tri_mask, 1.0, 0.0).astype(BF16)
    b = (jnp.dot(tri, hi, preferred_element_type=F32) + jnp.dot(tri, lo, preferred_element_type=F32))
    btot = b[tot_row:tot_row + 1, :]
    q = q_ref[sl, :]
    k = k_ref[sl, :]
    v = v_ref[sl, :].astype(BF16)
    qd = (q * (jnp.exp(b) * (GLA_DK ** -0.5))).astype(BF16)
    kd = (k * jnp.exp(-b)).astype(BF16)
    kr = (k * jnp.exp(btot - b)).astype(BF16)
    att = lax.dot_general(qd, kd, NT, preferred_element_type=F32)
    att = jnp.where(tri_mask, att, 0.0).astype(BF16)
    st = st_ref[...]
    o = (jnp.dot(att, v, preferred_element_type=F32)
         + lax.dot_general(qd, st.astype(BF16), NT, preferred_element_type=F32))
    o_ref[sl, :] = o
    st_ref[...] = jnp.exp(btot) * st + lax.dot_general(v, kr, TN, preferred_element_type=F32)


def _gla_body(qf, kf, vf, gdf, qb, kb, vb, gdb, upf, bsf, upb, bsb, of_ref, ob_ref,
              stf, stb, laf, lab, *, nchunk):
    @pl.when(pl.program_id(2) == 0)
    def _():
        stf[...] = jnp.zeros_like(stf)
        stb[...] = jnp.zeros_like(stb)

    laf[...] = _log_gate(gdf[:, 0:GLA_RANK], upf, bsf)
    lab[...] = _log_gate(gdb[:, GLA_RANK:2 * GLA_RANK], upb, bsb)
    C = GLA_CHUNK
    row = lax.broadcasted_iota(I32, (C, C), 0)
    col = lax.broadcasted_iota(I32, (C, C), 1)
    lower = row >= col
    upper = col >= row

    def step(c, carry):
        _gla_chunk(qf, kf, vf, laf, stf, of_ref, c * C, lower, C - 1)
        _gla_chunk(qb, kb, vb, lab, stb, ob_ref, (nchunk - 1 - c) * C, upper, 0)
        return carry

    lax.fori_loop(0, nchunk, step, 0, unroll=4)


def _gla(main3, gd3, up_f, bias_f, up_b, bias_b, lb=512):
    B, S, _ = main3.shape
    nb = S // lb
    kq = GLA_QK // GLA_DK
    kv = (2 * GLA_QK) // GLA_DV
    fwd = lambda b, h, i: (b, i, h)
    blk = lambda c0: (lambda b, h, i: (b, i, c0 + h))
    rblk = lambda c0: (lambda b, h, i: (b, nb - 1 - i, c0 + h))
    sq = lambda m: pl.BlockSpec((None, lb, GLA_DK), m)
    sv = lambda m: pl.BlockSpec((None, lb, GLA_DV), m)
    sg = lambda m: pl.BlockSpec((None, lb, GD_W), m)
    up_spec = pl.BlockSpec((GLA_RANK, GLA_DK), lambda b, h, i: (0, h))
    bias_spec = pl.BlockSpec((1, GLA_DK), lambda b, h, i: (0, h))
    out_sds = jax.ShapeDtypeStruct((B, S, GLA_V), F32)
    return pl.pallas_call(
        functools.partial(_gla_body, nchunk=lb // GLA_CHUNK),
        out_shape=(out_sds, out_sds),
        grid=(B, GLA_HEADS, nb),
        in_specs=[sq(blk(0)), sq(blk(kq)), sv(blk(kv)), sg(lambda b, h, i: (b, i, 0)),
                  sq(rblk(0)), sq(rblk(kq)), sv(rblk(kv)), sg(lambda b, h, i: (b, nb - 1 - i, 0)),
                  up_spec, bias_spec, up_spec, bias_spec],
        out_specs=(pl.BlockSpec((None, lb, GLA_DV), fwd),
                   pl.BlockSpec((None, lb, GLA_DV), lambda b, h, i: (b, nb - 1 - i, h))),
        scratch_shapes=[pltpu.VMEM((GLA_DV, GLA_DK), F32), pltpu.VMEM((GLA_DV, GLA_DK), F32),
                        pltpu.VMEM((lb, GLA_DK), F32), pltpu.VMEM((lb, GLA_DK), F32)],
        compiler_params=_cparams(3),
        name="gla",
    )(main3, main3, main3, gd3, main3, main3, main3, gd3, up_f, bias_f, up_b, bias_b)


ATT_TQ = 128
ATT_WIN = 256


ATT_GROUP = 4


ATT_FACTOR = 4
assert all(b == a * ATT_FACTOR for a, b in zip(ATT_DILATIONS[:-1], ATT_DILATIONS[1:])) and ATT_DILATIONS[0] == 1


def _split_classes(src, dst, d, n):
    f = ATT_FACTOR
    sub = n // (d * f)
    for r in range(d):
        for a in range(f):
            dst[pl.ds((d * a + r) * sub, sub), :] = src[pl.ds(r * (n // d) + a, sub, stride=f), :]


def _merge_classes(src, dst, d, n):
    f = ATT_FACTOR
    sub = n // (d * f)
    for r in range(d):
        for a in range(f):
            dst[pl.ds(r * (n // d) + a, sub, stride=f), :] = src[pl.ds((d * a + r) * sub, sub), :]


def _attn_body(slopes_ref, q_ref, k_ref, v_ref, o_ref, *scr, seq):
    pair = pl.program_id(1)
    lane = lax.broadcasted_iota(I32, (1, LANES), 1)
    head0 = lane < ATT_HD
    ii = lax.broadcasted_iota(I32, (ATT_TQ, ATT_WIN), 0)
    jj = lax.broadcasted_iota(I32, (ATT_TQ, ATT_WIN), 1)
    slope = (slopes_ref[2 * pair], slopes_ref[2 * pair + 1])
    npat = len(ATT_DILATIONS)
    copies = [(q_ref, k_ref, v_ref)] + [scr[3 * i:3 * i + 3] for i in range(npat - 1)]
    stats = [scr[3 * (npat - 1):3 * npat], scr[3 * npat:3 * (npat + 1)]]
    bias_s = scr[3 * (npat + 1)]
    for pi in range(1, npat):
        for x in range(3):
            _split_classes(copies[pi - 1][x], copies[pi][x], ATT_DILATIONS[pi - 1], seq)

    for step, pi in enumerate(reversed(range(npat))):
        d = ATT_DILATIONS[pi]
        L = seq // d
        nblk = L // ATT_TQ
        first = step == 0
        q_ref, k_ref, v_ref = copies[pi]
        if not first:
            for x in range(3):
                _merge_classes(stats[(step - 1) % 2][x], stats[step % 2][x], d, seq)
        m_s, l_s, acc_s = stats[step % 2]
        for case in range(3):
            ad = jnp.abs(jj - ii - case * ATT_RADIUS)
            for hh in range(2):
                bias_s[case, hh] = jnp.where(ad <= ATT_RADIUS, -slope[hh] * (ad * d).astype(F32), NEG)

        def group(gi, carry, L=L, nblk=nblk, first=first, q_ref=q_ref, k_ref=k_ref, v_ref=v_ref,
                  m_s=m_s, l_s=l_s, acc_s=acc_s):
            loaded = []
            for u in range(ATT_GROUP):
                t = gi * ATT_GROUP + u
                base = (t // nblk) * L
                q0 = (t % nblk) * ATT_TQ
                start = jnp.clip(q0 - ATT_RADIUS, 0, L - ATT_WIN)
                case = (q0 - start) // ATT_RADIUS
                qsl = pl.ds(pl.multiple_of(base + q0, ATT_TQ), ATT_TQ)
                ksl = pl.ds(pl.multiple_of(base + start, ATT_RADIUS), ATT_WIN)
                q = q_ref[qsl, :] * (ATT_HD ** -0.5)
                k = k_ref[ksl, :].astype(BF16)
                v = v_ref[ksl, :].astype(BF16)
                run = None if first else (m_s[qsl, :], l_s[qsl, :], acc_s[qsl, :])
                loaded.append((qsl, case, q, k, v, run))
            results = []
            for qsl, case, q, k, v, run in loaded:
                m_new, lsum, pv = [], [], []
                for hh in range(2):
                    hm = head0 if hh == 0 else jnp.logical_not(head0)
                    qh = jnp.where(hm, q, 0.0).astype(BF16)
                    s = lax.dot_general(qh, k, NT, preferred_element_type=F32) + bias_s[case, hh]
                    mh = jnp.max(s, axis=-1, keepdims=True)
                    p = jnp.exp(s - mh)
                    m_new.append(mh)
                    lsum.append(jnp.sum(p, axis=-1, keepdims=True))
                    pv.append(jnp.dot(p.astype(BF16), v, preferred_element_type=F32))
                m_full = jnp.where(head0, m_new[0], m_new[1])
                l_blk = jnp.where(head0, lsum[0], lsum[1])
                pv_blk = jnp.where(head0, pv[0], pv[1])
                if not first:
                    m_blk = m_full
                    m_full = jnp.maximum(run[0], m_blk)
                    a_run = jnp.exp(run[0] - m_full)
                    a_blk = jnp.exp(m_blk - m_full)
                    l_blk = a_run * run[1] + a_blk * l_blk
                    pv_blk = a_run * run[2] + a_blk * pv_blk
                results.append((qsl, m_full, l_blk, pv_blk))
            for qsl, m_full, l_blk, pv_blk in results:
                m_s[qsl, :] = m_full
                l_s[qsl, :] = l_blk
                acc_s[qsl, :] = pv_blk
            return carry

        lax.fori_loop(0, d * nblk // ATT_GROUP, group, 0)

    o_ref[...] = acc_s[...] / l_s[...]


def _attn(main3, slopes):
    B, S, _ = main3.shape
    c0 = (2 * GLA_QK + 2 * GLA_V) // LANES
    npair = ATT_HEADS // 2
    spec = lambda off: pl.BlockSpec((None, S, LANES), lambda b, p: (b, 0, c0 + off + p))
    return pl.pallas_call(
        functools.partial(_attn_body, seq=S),
        out_shape=jax.ShapeDtypeStruct((B, S, ATT_W), F32),
        grid=(B, npair),
        in_specs=[pl.BlockSpec(memory_space=pltpu.SMEM), spec(0), spec(npair), spec(2 * npair)],
        out_specs=pl.BlockSpec((None, S, LANES), lambda b, p: (b, 0, p)),
        scratch_shapes=([pltpu.VMEM((S, LANES), F32)] * (3 * (len(ATT_DILATIONS) + 1))
                        + [pltpu.VMEM((3, 2, ATT_TQ, ATT_WIN), F32)]),
        compiler_params=_cparams(2),
        name="attn",
    )(slopes, main3, main3, main3)


def _mix_body(of_ref, ob_ref, gr_ref, att_ref, x_ref, gg_ref, ag_ref, wo_ref, n2_ref, h_ref, hn_ref):
    o = of_ref[...] + ob_ref[...]
    parts = [_rms(o[:, h * GLA_DV:(h + 1) * GLA_DV], gg_ref[...]) for h in range(GLA_HEADS)]
    on = jnp.concatenate(parts, axis=-1)
    gr = gr_ref[...]
    gla = on * (gr / (1.0 + jnp.exp(-gr)))
    an = _rms(att_ref[...], ag_ref[...])
    y = (jnp.dot(gla.astype(BF16), wo_ref[0:GLA_V, :], preferred_element_type=F32)
         + jnp.dot(an.astype(BF16), wo_ref[GLA_V:GLA_V + ATT_W, :], preferred_element_type=F32))
    h = x_ref[...] + y
    h_ref[...] = h
    hn_ref[...] = _rms(h, n2_ref[...]).astype(BF16)


def _mix(o_f, o_b, main2, att2, x2, gg, ag, w_out, n2g, tm=512):
    T = x2.shape[0]
    gr_blk = (2 * GLA_QK + GLA_V) // GLA_V
    row = lambda i: (i, 0)
    const = lambda i: (0, 0)
    return pl.pallas_call(
        _mix_body,
        out_shape=(jax.ShapeDtypeStruct((T, D_MODEL), F32), jax.ShapeDtypeStruct((T, D_MODEL), BF16)),
        grid=(T // tm,),
        in_specs=[pl.BlockSpec((tm, GLA_V), row), pl.BlockSpec((tm, GLA_V), row),
                  pl.BlockSpec((tm, GLA_V), lambda i: (i, gr_blk)),
                  pl.BlockSpec((tm, ATT_W), row), pl.BlockSpec((tm, D_MODEL), row),
                  pl.BlockSpec((1, GLA_DV), const), pl.BlockSpec((1, ATT_W), const),
                  pl.BlockSpec((D_MODEL, D_MODEL), const), pl.BlockSpec((1, D_MODEL), const)],
        out_specs=(pl.BlockSpec((tm, D_MODEL), row), pl.BlockSpec((tm, D_MODEL), row)),
        compiler_params=_cparams(1),
        name="mix",
    )(o_f, o_b, main2, att2, x2, gg, ag, w_out, n2g)


ROUTE_TM = SUBLANES * LANES
_CELLS = [(a, b) for a in range(PEER_TOPK) for b in range(PEER_TOPK) if (a + 1) * (b + 1) <= PEER_TOPK]


def _key_rows(i):
    return pl.ds(i * SUBLANES, SUBLANES)


def _before(a, b):
    (va, ia), (vb, ib) = a, b
    if isinstance(ia, float) and isinstance(ib, float):
        return (va >= vb) if ia < ib else (va > vb)
    return jnp.logical_or(va > vb, jnp.logical_and(va == vb, ia < ib))


def _pick(c, a, b):
    return tuple(jnp.where(c, x, y) for x, y in zip(a, b))


def _ordered_pair(a, b):
    c = _before(a, b)
    return _pick(c, a, b), _pick(c, b, a)


def _bitonic_merge(xs):
    n = len(xs)
    if n == 1:
        return list(xs)
    h = n // 2
    xs = list(xs)
    for k in range(h):
        xs[k], xs[k + h] = _ordered_pair(xs[k], xs[k + h])
    return _bitonic_merge(xs[:h]) + _bitonic_merge(xs[h:])


def _bitonic_sort(xs):
    n = len(xs)
    if n == 1:
        return list(xs)
    return _bitonic_merge(_bitonic_sort(xs[:n // 2]) + _bitonic_sort(xs[n // 2:])[::-1])


def _top_keys(s_ref):
    k = PEER_TOPK
    best = None
    for g in range(PEER_NKEYS // k):
        grp = _bitonic_sort([(s_ref[_key_rows(g * k + a), :], float(g * k + a)) for a in range(k)])
        grp = [(v, jnp.full((SUBLANES, LANES), i, F32) if isinstance(i, float) else i) for v, i in grp]
        if best is None:
            best = grp
        else:
            best = _bitonic_merge([_pick(_before(best[j], grp[k - 1 - j]), best[j], grp[k - 1 - j])
                                   for j in range(k)])
    return [v for v, _ in best], [i for _, i in best]


def _to_token_lanes(src_ref, dst_ref):
    for g in range(SUBLANES):
        dst_ref[g] = src_ref[pl.ds(g, PEER_NKEYS, stride=SUBLANES), :].astype(dst_ref.dtype)


def _route_body(hn_ref, wq_ref, keys_ref, r2_ref, e2_ref, n1_ref, c1_ref, s1_s, s2_s, rk2_s, n1_s):
    q = jnp.dot(hn_ref[...], wq_ref[...], preferred_element_type=F32)
    for c, s_s in enumerate((s1_s, s2_s)):
        qc = q[:, c * PEER_HALF:(c + 1) * PEER_HALF].astype(BF16)
        st = lax.dot_general(keys_ref[c], qc, NT, preferred_element_type=F32)
        for g in range(SUBLANES):
            s_s[pl.ds(g, PEER_NKEYS, stride=SUBLANES), :] = st[:, g * LANES:(g + 1) * LANES]
    sc1, si1 = _top_keys(s1_s)
    sc2, si2 = _top_keys(s2_s)
    e1 = [jnp.exp(sc1[k] - sc1[0]) for k in range(PEER_TOPK)]
    e2 = [jnp.exp(sc2[k] - sc2[0]) for k in range(PEER_TOPK)]

    cand = [sc1[a] + sc2[b] for (a, b) in _CELLS]
    nc = len(_CELLS)
    before = [jnp.zeros((SUBLANES, LANES), F32) for _ in range(nc)]
    for x in range(nc):
        for y in range(x + 1, nc):
            ax, bx = _CELLS[x]
            ay, by = _CELLS[y]
            if ax <= ay and bx <= by:
                before[y] = before[y] + 1.0
                continue
            bt = jnp.where(cand[x] >= cand[y], 1.0, 0.0)
            before[y] = before[y] + bt
            before[x] = before[x] + (1.0 - bt)
    zero = jnp.zeros((SUBLANES, LANES), F32)
    ncol = [zero] * PEER_TOPK
    zsum = zero
    for x, (a, b) in enumerate(_CELLS):
        sel = jnp.where(before[x] < float(PEER_TOPK), 1.0, 0.0)
        ncol[a] = ncol[a] + sel
        zsum = zsum + sel * (e1[a] * e2[b])
    inv_z = 1.0 / zsum

    def dense(i, carry):
        rows = _key_rows(i)
        key = lax.convert_element_type(i, F32)
        n1 = zero
        rk2 = jnp.full((SUBLANES, LANES), float(PEER_TOPK), F32)
        for k in range(PEER_TOPK):
            n1 = jnp.where(si1[k] == key, ncol[k], n1)
            rk2 = jnp.where(si2[k] == key, float(k), rk2)
        n1_s[rows, :] = n1
        rk2_s[rows, :] = rk2
        s1_s[rows, :] = jnp.exp(s1_s[rows, :] - sc1[0]) * inv_z
        s2_s[rows, :] = jnp.exp(s2_s[rows, :] - sc2[0])
        return carry

    lax.fori_loop(0, PEER_NKEYS, dense, 0)
    _to_token_lanes(rk2_s, r2_ref)
    _to_token_lanes(s2_s, e2_ref)
    _to_token_lanes(n1_s, n1_ref)
    _to_token_lanes(s1_s, c1_ref)


def _route(hn, w_q, keys):
    T = hn.shape[0]
    tm = ROUTE_TM
    out_sds = lambda dt: jax.ShapeDtypeStruct((PEER_HEADS, T // LANES, PEER_NKEYS, LANES), dt)
    out_spec = pl.BlockSpec((None, SUBLANES, PEER_NKEYS, LANES), lambda i, h: (h, i, 0, 0))
    km = pltpu.VMEM((PEER_NKEYS * SUBLANES, LANES), F32)
    return pl.pallas_call(
        _route_body,
        out_shape=(out_sds(F32),) * 4,
        grid=(T // tm, PEER_HEADS),
        in_specs=[pl.BlockSpec((tm, D_MODEL), lambda i, h: (i, 0)),
                  pl.BlockSpec((D_MODEL, 2 * PEER_HALF), lambda i, h: (0, h)),
                  pl.BlockSpec((None, 2, PEER_NKEYS, PEER_HALF), lambda i, h: (h, 0, 0, 0))],
        out_specs=(out_spec,) * 4,
        scratch_shapes=[km] * 4,
        compiler_params=_cparams(2),
        name="route",
    )(hn, w_q, keys)


PEER_TE = 1024
PEER_QROWS = 32
INV_SQRT2 = 0.7071067811865476


def _peer_body(hn_ref, u_ref, vt_ref, r2_ref, e2_ref, n1_ref, c1_ref, o_ref, a_scr, g_scr):
    j = pl.program_id(1)

    @pl.when(j == 0)
    def _():
        o_ref[...] = jnp.zeros_like(o_ref)

    n_il = PEER_TE // PEER_NKEYS
    reps = PEER_QROWS // SUBLANES
    a_scr[...] = lax.dot_general(u_ref[...], hn_ref[...], NT, preferred_element_type=F32)
    for il0 in range(0, n_il, 2):
        ils = (il0, il0 + 1)
        for cb in range(g_scr.shape[1] // LANES):
            lanes = slice(cb * LANES, (cb + 1) * LANES)
            for qd in range(PEER_NKEYS // PEER_QROWS):
                rows = slice(qd * PEER_QROWS, (qd + 1) * PEER_QROWS)
                w = {il: None for il in ils}
                for h in range(PEER_HEADS):
                    r2 = r2_ref[h, cb, rows, :]
                    e2 = e2_ref[h, cb, rows, :]
                    for il in ils:
                        key = pl.ds(j * n_il + il, SUBLANES, stride=0)
                        n1 = jnp.concatenate([n1_ref[h, cb, key, :]] * reps, axis=0)
                        c1 = jnp.concatenate([c1_ref[h, cb, key, :]] * reps, axis=0)
                        t = jnp.where(r2 < n1, e2, 0.0) * c1
                        w[il] = t if w[il] is None else w[il] + t
                for il in ils:
                    er = slice(il * PEER_NKEYS + qd * PEER_QROWS, il * PEER_NKEYS + (qd + 1) * PEER_QROWS)
                    a = a_scr[er, lanes]
                    act = 0.5 * a * (1.0 + lax.erf(a * INV_SQRT2))
                    g_scr[er, lanes] = (act * w[il]).astype(BF16)
    o_ref[...] += jnp.dot(vt_ref[...], g_scr[...], preferred_element_type=F32)


def _peer(hn, u, vt, r2, e2, n1, c1, tm=512):
    T = hn.shape[0]
    te = PEER_TE
    once = pl.Buffered(1)
    rspec = pl.BlockSpec((PEER_HEADS, tm // LANES, PEER_NKEYS, LANES), lambda i, j: (0, i, 0, 0), pipeline_mode=once)
    return pl.pallas_call(
        _peer_body,
        out_shape=jax.ShapeDtypeStruct((D_MODEL, T), F32),
        grid=(T // tm, PEER_EXPERTS // te),
        in_specs=[pl.BlockSpec((tm, D_MODEL), lambda i, j: (i, 0), pipeline_mode=once),
                  pl.BlockSpec((te, D_MODEL), lambda i, j: (j, 0)),
                  pl.BlockSpec((None, D_MODEL, te), lambda i, j: (j, 0, 0)),
                  rspec, rspec, rspec, rspec],
        out_specs=pl.BlockSpec((D_MODEL, tm), lambda i, j: (0, i)),
        scratch_shapes=[pltpu.VMEM((te, tm), F32), pltpu.VMEM((te, tm), BF16)],
        compiler_params=_cparams(2),
        name="peer",
    )(hn, u, vt, r2, e2, n1, c1)


def _final_body(h_ref, pt_ref, g_ref, o_ref):
    o_ref[...] = _rms(h_ref[...] + pt_ref[...].T, g_ref[...])


def _final(h, peer_t, g, tm=512):
    T = h.shape[0]
    return pl.pallas_call(
        _final_body,
        out_shape=jax.ShapeDtypeStruct((T, D_MODEL), F32),
        grid=(T // tm,),
        in_specs=[pl.BlockSpec((tm, D_MODEL), lambda i: (i, 0)),
                  pl.BlockSpec((D_MODEL, tm), lambda i: (0, i)),
                  pl.BlockSpec((1, D_MODEL), lambda i: (0, 0))],
        out_specs=pl.BlockSpec((tm, D_MODEL), lambda i: (i, 0)),
        compiler_params=_cparams(1),
        name="final",
    )(h, peer_t, g)


def _layer(x, norm1_g, w_in, up_f, bias_f, up_b, bias_b, gla_norm_g, att_norm_g, w_out, norm2_g,
           w_q, sub_keys, peer_u, peer_v):
    B, S, D = x.shape
    T = B * S
    x2 = x.reshape(T, D)
    gd0 = 2 * GLA_QK + 2 * GLA_V
    w_main = jnp.concatenate([w_in[:, :gd0], w_in[:, gd0 + GD_W:]], axis=1).astype(BF16)
    w_gd = w_in[:, gd0:gd0 + GD_W].astype(BF16)
    main, gd = _inproj(x2, norm1_g.reshape(1, D), w_main, w_gd)
    main3 = main.reshape(B, S, MAIN_W)
    o_f, o_b = _gla(main3, gd.reshape(B, S, GD_W), up_f, bias_f.reshape(1, GLA_QK),
                    up_b, bias_b.reshape(1, GLA_QK))
    slopes = jnp.asarray((2.0 ** (-8.0 * np.arange(1, ATT_HEADS + 1) / ATT_HEADS)).astype(np.float32))
    att = _attn(main3, slopes)
    h, hn = _mix(o_f.reshape(T, GLA_V), o_b.reshape(T, GLA_V), main, att.reshape(T, ATT_W), x2,
                 gla_norm_g.reshape(1, GLA_DV), att_norm_g.reshape(1, ATT_W), w_out.astype(BF16),
                 norm2_g.reshape(1, D))
    r2, e2, n1, c1 = _route(hn, w_q.astype(BF16), sub_keys.astype(BF16))
    vt = peer_v.astype(BF16).reshape(PEER_EXPERTS // PEER_TE, PEER_TE, D).transpose(0, 2, 1)
    peer_t = _peer(hn, peer_u.astype(BF16), vt, r2, e2, n1, c1)
    return h, peer_t


def kernel(x, norm1_g, w_in, gla_gate_up_f, gla_gate_bias_f, gla_gate_up_b, gla_gate_bias_b, gla_norm_g,
           att_norm_g, w_out, norm2_g, peer_w_q, peer_sub_keys, peer_u, peer_v, final_norm_g):
    B, S, D = x.shape
    assert norm1_g.shape[0] == 1, "single trunk layer: the final norm is fused with the layer's last residual"
    h, peer_t = _layer(x, norm1_g[0], w_in[0], gla_gate_up_f[0], gla_gate_bias_f[0], gla_gate_up_b[0],
                       gla_gate_bias_b[0], gla_norm_g[0], att_norm_g[0], w_out[0], norm2_g[0],
                       peer_w_q[0], peer_sub_keys[0], peer_u[0], peer_v[0])
    return _final(h, peer_t, final_norm_g.reshape(1, D)).reshape(B, S, D)
```

```python
import functools

import numpy as np
import jax
import jax.numpy as jnp
from jax import lax
from jax.experimental import pallas as pl
from jax.experimental.pallas import tpu as pltpu

F32 = jnp.float32
BF16 = jnp.bfloat16
I32 = jnp.int32

D_MODEL = 2048
GLA_HEADS = 4
GLA_DK = 128
GLA_DV = 256
GLA_RANK = 16
GLA_TAU = 16.0
GLA_CHUNK = 64
GLA_QK = GLA_HEADS * GLA_DK
GLA_V = GLA_HEADS * GLA_DV
ATT_HEADS = 16
ATT_HD = 64
ATT_W = ATT_HEADS * ATT_HD
ATT_DILATIONS = (1, 4, 16)
ATT_RADIUS = 64
PEER_NKEYS = 128
PEER_HEADS = 8
PEER_TOPK = 16
PEER_EXPERTS = PEER_NKEYS * PEER_NKEYS
PEER_HALF = 128
EPS = 1e-6
NEG = -1e30
MAIN_W = 2 * GLA_QK + 2 * GLA_V + 3 * ATT_W
GD_W = 2 * GLA_RANK

LANES = 128
SUBLANES = 8
VMEM_LIMIT = 56 * 1024 * 1024

NT = (((1,), (1,)), ((), ()))
TN = (((0,), (0,)), ((), ()))


def _cparams(n_axes, flags=None):
    return pltpu.CompilerParams(dimension_semantics=("arbitrary",) * n_axes,
                                vmem_limit_bytes=VMEM_LIMIT, flags=flags)


def _rms(x, g):
    ms = jnp.mean(x * x, axis=-1, keepdims=True)
    return x * lax.rsqrt(ms + EPS) * g


def _inproj_body(x_ref, g_ref, w_ref, wgd_ref, o_ref, gd_ref, xn_ref):
    @pl.when(pl.program_id(1) == 0)
    def _():
        xn = _rms(x_ref[...], g_ref[...]).astype(BF16)
        xn_ref[...] = xn
        gd_ref[...] = jnp.dot(xn, wgd_ref[...], preferred_element_type=F32)

    o_ref[...] = jnp.dot(xn_ref[...], w_ref[...], preferred_element_type=F32)


def _inproj(x2, g, w_main, w_gd, tm=1024, tn=1024):
    T = x2.shape[0]
    return pl.pallas_call(
        _inproj_body,
        out_shape=(jax.ShapeDtypeStruct((T, MAIN_W), F32), jax.ShapeDtypeStruct((T, GD_W), F32)),
        grid=(T // tm, MAIN_W // tn),
        in_specs=[pl.BlockSpec((tm, D_MODEL), lambda i, j: (i, 0)),
                  pl.BlockSpec((1, D_MODEL), lambda i, j: (0, 0)),
                  pl.BlockSpec((D_MODEL, tn), lambda i, j: (0, j)),
                  pl.BlockSpec((D_MODEL, GD_W), lambda i, j: (0, 0))],
        out_specs=(pl.BlockSpec((tm, tn), lambda i, j: (i, j)),
                   pl.BlockSpec((tm, GD_W), lambda i, j: (i, 0))),
        scratch_shapes=[pltpu.VMEM((tm, D_MODEL), BF16)],
        compiler_params=_cparams(2),
        name="inproj",
    )(x2, g, w_main, w_gd)


def _log_gate(gd, up_ref, bias_ref):
    z = jnp.dot(gd.astype(BF16), up_ref[...].astype(BF16), preferred_element_type=F32) + bias_ref[...]
    return (jnp.minimum(z, 0.0) - jnp.log1p(jnp.exp(-jnp.abs(z)))) * (1.0 / GLA_TAU)


def _gla_chunk(q_ref, k_ref, v_ref, la_ref, st_ref, o_ref, r0, tri_mask, tot_row):
    C = GLA_CHUNK
    sl = pl.ds(pl.multiple_of(r0, C), C)
    la = la_ref[sl, :]
    hi = la.astype(BF16)
    lo = (la - hi.astype(F32)).astype(BF16)
    tri = jnp.where(tri_mask, 1.0, 0.0).astype(BF16)
    b = (jnp.dot(tri, hi, preferred_element_type=F32) + jnp.dot(tri, lo, preferred_element_type=F32))
    btot = b[tot_row:tot_row + 1, :]
    q = q_ref[sl, :]
    k = k_ref[sl, :]
    v = v_ref[sl, :].astype(BF16)
    qd = (q * (jnp.exp(b) * (GLA_DK ** -0.5))).astype(BF16)
    kd = (k * jnp.exp(-b)).astype(BF16)
    kr = (k * jnp.exp(btot - b)).astype(BF16)
    att = lax.dot_general(qd, kd, NT, preferred_element_type=F32)
    att = jnp.where(tri_mask, att, 0.0).astype(BF16)
    st = st_ref[...]
    o = (jnp.dot(att, v, preferred_element_type=F32)
         + lax.dot_general(qd, st.astype(BF16), NT, preferred_element_type=F32))
    o_ref[sl, :] = o
    st_ref[...] = jnp.exp(btot) * st + lax.dot_general(v, kr, TN, preferred_element_type=F32)


def _gla_body(qf, kf, vf, gdf, qb, kb, vb, gdb, upf, bsf, upb, bsb, of_ref, ob_ref,
              stf, stb, laf, lab, *, nchunk):
    @pl.when(pl.program_id(2) == 0)
    def _():
        stf[...] = jnp.zeros_like(stf)
        stb[...] = jnp.zeros_like(stb)

    laf[...] = _log_gate(gdf[:, 0:GLA_RANK], upf, bsf)
    lab[...] = _log_gate(gdb[:, GLA_RANK:2 * GLA_RANK], upb, bsb)
    C = GLA_CHUNK
    row = lax.broadcasted_iota(I32, (C, C), 0)
    col = lax.broadcasted_iota(I32, (C, C), 1)
    lower = row >= col
    upper = col >= row

    def step(c, carry):
        _gla_chunk(qf, kf, vf, laf, stf, of_ref, c * C, lower, C - 1)
        _gla_chunk(qb, kb, vb, lab, stb, ob_ref, (nchunk - 1 - c) * C, upper, 0)
        return carry

    lax.fori_loop(0, nchunk, step, 0, unroll=4)


def _gla(main3, gd3, up_f, bias_f, up_b, bias_b, lb=512):
    B, S, _ = main3.shape
    nb = S // lb
    kq = GLA_QK // GLA_DK
    kv = (2 * GLA_QK) // GLA_DV
    fwd = lambda b, h, i: (b, i, h)
    blk = lambda c0: (lambda b, h, i: (b, i, c0 + h))
    rblk = lambda c0: (lambda b, h, i: (b, nb - 1 - i, c0 + h))
    sq = lambda m: pl.BlockSpec((None, lb, GLA_DK), m)
    sv = lambda m: pl.BlockSpec((None, lb, GLA_DV), m)
    sg = lambda m: pl.BlockSpec((None, lb, GD_W), m)
    up_spec = pl.BlockSpec((GLA_RANK, GLA_DK), lambda b, h, i: (0, h))
    bias_spec = pl.BlockSpec((1, GLA_DK), lambda b, h, i: (0, h))
    out_sds = jax.ShapeDtypeStruct((B, S, GLA_V), F32)
    return pl.pallas_call(
        functools.partial(_gla_body, nchunk=lb // GLA_CHUNK),
        out_shape=(out_sds, out_sds),
        grid=(B, GLA_HEADS, nb),
        in_specs=[sq(blk(0)), sq(blk(kq)), sv(blk(kv)), sg(lambda b, h, i: (b, i, 0)),
                  sq(rblk(0)), sq(rblk(kq)), sv(rblk(kv)), sg(lambda b, h, i: (b, nb - 1 - i, 0)),
                  up_spec, bias_spec, up_spec, bias_spec],
        out_specs=(pl.BlockSpec((None, lb, GLA_DV), fwd),
                   pl.BlockSpec((None, lb, GLA_DV), lambda b, h, i: (b, nb - 1 - i, h))),
        scratch_shapes=[pltpu.VMEM((GLA_DV, GLA_DK), F32), pltpu.VMEM((GLA_DV, GLA_DK), F32),
                        pltpu.VMEM((lb, GLA_DK), F32), pltpu.VMEM((lb, GLA_DK), F32)],
        compiler_params=_cparams(3),
        name="gla",
    )(main3, main3, main3, gd3, main3, main3, main3, gd3, up_f, bias_f, up_b, bias_b)


ATT_TQ = 128
ATT_WIN = 256


ATT_GROUP = 4


ATT_FACTOR = 4
assert all(b == a * ATT_FACTOR for a, b in zip(ATT_DILATIONS[:-1], ATT_DILATIONS[1:])) and ATT_DILATIONS[0] == 1


def _split_classes(src, dst, d, n):
    f = ATT_FACTOR
    sub = n // (d * f)
    for r in range(d):
        for a in range(f):
            dst[pl.ds((d * a + r) * sub, sub), :] = src[pl.ds(r * (n // d) + a, sub, stride=f), :]


def _merge_classes(src, dst, d, n):
    f = ATT_FACTOR
    sub = n // (d * f)
    for r in range(d):
        for a in range(f):
            dst[pl.ds(r * (n // d) + a, sub, stride=f), :] = src[pl.ds((d * a + r) * sub, sub), :]


def _attn_body(slopes_ref, q_ref, k_ref, v_ref, o_ref, *scr, seq):
    pair = pl.program_id(1)
    lane = lax.broadcasted_iota(I32, (1, LANES), 1)
    head0 = lane < ATT_HD
    ii = lax.broadcasted_iota(I32, (ATT_TQ, ATT_WIN), 0)
    jj = lax.broadcasted_iota(I32, (ATT_TQ, ATT_WIN), 1)
    slope = (slopes_ref[2 * pair], slopes_ref[2 * pair + 1])
    npat = len(ATT_DILATIONS)
    copies = [(q_ref, k_ref, v_ref)] + [scr[3 * i:3 * i + 3] for i in range(npat - 1)]
    stats = [scr[3 * (npat - 1):3 * npat], scr[3 * npat:3 * (npat + 1)]]
    bias_s = scr[3 * (npat + 1)]
    for pi in range(1, npat):
        for x in range(3):
            _split_classes(copies[pi - 1][x], copies[pi][x], ATT_DILATIONS[pi - 1], seq)

    for step, pi in enumerate(reversed(range(npat))):
        d = ATT_DILATIONS[pi]
        L = seq // d
        nblk = L // ATT_TQ
        first = step == 0
        q_ref, k_ref, v_ref = copies[pi]
        if not first:
            for x in range(3):
                _merge_classes(stats[(step - 1) % 2][x], stats[step % 2][x], d, seq)
        m_s, l_s, acc_s = stats[step % 2]
        for case in range(3):
            ad = jnp.abs(jj - ii - case * ATT_RADIUS)
            for hh in range(2):
                bias_s[case, hh] = jnp.where(ad <= ATT_RADIUS, -slope[hh] * (ad * d).astype(F32), NEG)

        def group(gi, carry, L=L, nblk=nblk, first=first, q_ref=q_ref, k_ref=k_ref, v_ref=v_ref,
                  m_s=m_s, l_s=l_s, acc_s=acc_s):
            loaded = []
            for u in range(ATT_GROUP):
                t = gi * ATT_GROUP + u
                base = (t // nblk) * L
                q0 = (t % nblk) * ATT_TQ
                start = jnp.clip(q0 - ATT_RADIUS, 0, L - ATT_WIN)
                case = (q0 - start) // ATT_RADIUS
                qsl = pl.ds(pl.multiple_of(base + q0, ATT_TQ), ATT_TQ)
                ksl = pl.ds(pl.multiple_of(base + start, ATT_RADIUS), ATT_WIN)
                q = q_ref[qsl, :] * (ATT_HD ** -0.5)
                k = k_ref[ksl, :].astype(BF16)
                v = v_ref[ksl, :].astype(BF16)
                run = None if first else (m_s[qsl, :], l_s[qsl, :], acc_s[qsl, :])
                loaded.append((qsl, case, q, k, v, run))
            results = []
            for qsl, case, q, k, v, run in loaded:
                m_new, lsum, pv = [], [], []
                for hh in range(2):
                    hm = head0 if hh == 0 else jnp.logical_not(head0)
                    qh = jnp.where(hm, q, 0.0).astype(BF16)
                    s = lax.dot_general(qh, k, NT, preferred_element_type=F32) + bias_s[case, hh]
                    mh = jnp.max(s, axis=-1, keepdims=True)
                    p = jnp.exp(s - mh)
                    m_new.append(mh)
                    lsum.append(jnp.sum(p, axis=-1, keepdims=True))
                    pv.append(jnp.dot(p.astype(BF16), v, preferred_element_type=F32))
                m_full = jnp.where(head0, m_new[0], m_new[1])
                l_blk = jnp.where(head0, lsum[0], lsum[1])
                pv_blk = jnp.where(head0, pv[0], pv[1])
                if not first:
                    m_blk = m_full
                    m_full = jnp.maximum(run[0], m_blk)
                    a_run = jnp.exp(run[0] - m_full)
                    a_blk = jnp.exp(m_blk - m_full)
                    l_blk = a_run * run[1] + a_blk * l_blk
                    pv_blk = a_run * run[2] + a_blk * pv_blk
                results.append((qsl, m_full, l_blk, pv_blk))
            for qsl, m_full, l_blk, pv_blk in results:
                m_s[qsl, :] = m_full
                l_s[qsl, :] = l_blk
                acc_s[qsl, :] = pv_blk
            return carry

        lax.fori_loop(0, d * nblk // ATT_GROUP, group, 0)

    o_ref[...] = acc_s[...] / l_s[...]


def _attn(main3, slopes):
    B, S, _ = main3.shape
    c0 = (2 * GLA_QK + 2 * GLA_V) // LANES
    npair = ATT_HEADS // 2
    spec = lambda off: pl.BlockSpec((None, S, LANES), lambda b, p: (b, 0, c0 + off + p))
    return pl.pallas_call(
        functools.partial(_attn_body, seq=S),
        out_shape=jax.ShapeDtypeStruct((B, S, ATT_W), F32),
        grid=(B, npair),
        in_specs=[pl.BlockSpec(memory_space=pltpu.SMEM), spec(0), spec(npair), spec(2 * npair)],
        out_specs=pl.BlockSpec((None, S, LANES), lambda b, p: (b, 0, p)),
        scratch_shapes=([pltpu.VMEM((S, LANES), F32)] * (3 * (len(ATT_DILATIONS) + 1))
                        + [pltpu.VMEM((3, 2, ATT_TQ, ATT_WIN), F32)]),
        compiler_params=_cparams(2),
        name="attn",
    )(slopes, main3, main3, main3)


def _mix_body(of_ref, ob_ref, gr_ref, att_ref, x_ref, gg_ref, ag_ref, wo_ref, n2_ref, h_ref, hn_ref):
    o = of_ref[...] + ob_ref[...]
    parts = [_rms(o[:, h * GLA_DV:(h + 1) * GLA_DV], gg_ref[...]) for h in range(GLA_HEADS)]
    on = jnp.concatenate(parts, axis=-1)
    gr = gr_ref[...]
    gla = on * (gr / (1.0 + jnp.exp(-gr)))
    an = _rms(att_ref[...], ag_ref[...])
    y = (jnp.dot(gla.astype(BF16), wo_ref[0:GLA_V, :], preferred_element_type=F32)
         + jnp.dot(an.astype(BF16), wo_ref[GLA_V:GLA_V + ATT_W, :], preferred_element_type=F32))
    h = x_ref[...] + y
    h_ref[...] = h
    hn_ref[...] = _rms(h, n2_ref[...]).astype(BF16)


def _mix(o_f, o_b, main2, att2, x2, gg, ag, w_out, n2g, tm=512):
    T = x2.shape[0]
    gr_blk = (2 * GLA_QK + GLA_V) // GLA_V
    row = lambda i: (i, 0)
    const = lambda i: (0, 0)
    return pl.pallas_call(
        _mix_body,
        out_shape=(jax.ShapeDtypeStruct((T, D_MODEL), F32), jax.ShapeDtypeStruct((T, D_MODEL), BF16)),
        grid=(T // tm,),
        in_specs=[pl.BlockSpec((tm, GLA_V), row), pl.BlockSpec((tm, GLA_V), row),
                  pl.BlockSpec((tm, GLA_V), lambda i: (i, gr_blk)),
                  pl.BlockSpec((tm, ATT_W), row), pl.BlockSpec((tm, D_MODEL), row),
                  pl.BlockSpec((1, GLA_DV), const), pl.BlockSpec((1, ATT_W), const),
                  pl.BlockSpec((D_MODEL, D_MODEL), const), pl.BlockSpec((1, D_MODEL), const)],
        out_specs=(pl.BlockSpec((tm, D_MODEL), row), pl.BlockSpec((tm, D_MODEL), row)),
        compiler_params=_cparams(1),
        name="mix",
    )(o_f, o_b, main2, att2, x2, gg, ag, w_out, n2g)


ROUTE_TM = SUBLANES * LANES
_CELLS = [(a, b) for a in range(PEER_TOPK) for b in range(PEER_TOPK) if (a + 1) * (b + 1) <= PEER_TOPK]


def _key_rows(i):
    return pl.ds(i * SUBLANES, SUBLANES)


def _before(a, b):
    (va, ia), (vb, ib) = a, b
    if isinstance(ia, float) and isinstance(ib, float):
        return (va >= vb) if ia < ib else (va > vb)
    return jnp.logical_or(va > vb, jnp.logical_and(va == vb, ia < ib))


def _pick(c, a, b):
    return tuple(jnp.where(c, x, y) for x, y in zip(a, b))


def _ordered_pair(a, b):
    c = _before(a, b)
    return _pick(c, a, b), _pick(c, b, a)


def _bitonic_merge(xs):
    n = len(xs)
    if n == 1:
        return list(xs)
    h = n // 2
    xs = list(xs)
    for k in range(h):
        xs[k], xs[k + h] = _ordered_pair(xs[k], xs[k + h])
    return _bitonic_merge(xs[:h]) + _bitonic_merge(xs[h:])


def _bitonic_sort(xs):
    n = len(xs)
    if n == 1:
        return list(xs)
    return _bitonic_merge(_bitonic_sort(xs[:n // 2]) + _bitonic_sort(xs[n // 2:])[::-1])


def _top_keys(s_ref):
    k = PEER_TOPK
    best = None
    for g in range(PEER_NKEYS // k):
        grp = _bitonic_sort([(s_ref[_key_rows(g * k + a), :], float(g * k + a)) for a in range(k)])
        grp = [(v, jnp.full((SUBLANES, LANES), i, F32) if isinstance(i, float) else i) for v, i in grp]
        if best is None:
            best = grp
        else:
            best = _bitonic_merge([_pick(_before(best[j], grp[k - 1 - j]), best[j], grp[k - 1 - j])
                                   for j in range(k)])
    return [v for v, _ in best], [i for _, i in best]


def _to_token_lanes(src_ref, dst_ref):
    for g in range(SUBLANES):
        dst_ref[g] = src_ref[pl.ds(g, PEER_NKEYS, stride=SUBLANES), :].astype(dst_ref.dtype)


def _route_body(hn_ref, wq_ref, keys_ref, r2_ref, e2_ref, n1_ref, c1_ref, s1_s, s2_s, rk2_s, n1_s):
    q = jnp.dot(hn_ref[...], wq_ref[...], preferred_element_type=F32)
    for c, s_s in enumerate((s1_s, s2_s)):
        qc = q[:, c * PEER_HALF:(c + 1) * PEER_HALF].astype(BF16)
        st = lax.dot_general(keys_ref[c], qc, NT, preferred_element_type=F32)
        for g in range(SUBLANES):
            s_s[pl.ds(g, PEER_NKEYS, stride=SUBLANES), :] = st[:, g * LANES:(g + 1) * LANES]
    sc1, si1 = _top_keys(s1_s)
    sc2, si2 = _top_keys(s2_s)
    e1 = [jnp.exp(sc1[k] - sc1[0]) for k in range(PEER_TOPK)]
    e2 = [jnp.exp(sc2[k] - sc2[0]) for k in range(PEER_TOPK)]

    cand = [sc1[a] + sc2[b] for (a, b) in _CELLS]
    nc = len(_CELLS)
    before = [jnp.zeros((SUBLANES, LANES), F32) for _ in range(nc)]
    for x in range(nc):
        for y in range(x + 1, nc):
            ax, bx = _CELLS[x]
            ay, by = _CELLS[y]
            if ax <= ay and bx <= by:
                before[y] = before[y] + 1.0
                continue
            bt = jnp.where(cand[x] >= cand[y], 1.0, 0.0)
            before[y] = before[y] + bt
            before[x] = before[x] + (1.0 - bt)
    zero = jnp.zeros((SUBLANES, LANES), F32)
    ncol = [zero] * PEER_TOPK
    zsum = zero
    for x, (a, b) in enumerate(_CELLS):
        sel = jnp.where(before[x] < float(PEER_TOPK), 1.0, 0.0)
        ncol[a] = ncol[a] + sel
        zsum = zsum + sel * (e1[a] * e2[b])
    inv_z = 1.0 / zsum

    def dense(i, carry):
        rows = _key_rows(i)
        key = lax.convert_element_type(i, F32)
        n1 = zero
        rk2 = jnp.full((SUBLANES, LANES), float(PEER_TOPK), F32)
        for k in range(PEER_TOPK):
            n1 = jnp.where(si1[k] == key, ncol[k], n1)
            rk2 = jnp.where(si2[k] == key, float(k), rk2)
        n1_s[rows, :] = n1
        rk2_s[rows, :] = rk2
        s1_s[rows, :] = jnp.exp(s1_s[rows, :] - sc1[0]) * inv_z
        s2_s[rows, :] = jnp.exp(s2_s[rows, :] - sc2[0])
        return carry

    lax.fori_loop(0, PEER_NKEYS, dense, 0)
    _to_token_lanes(rk2_s, r2_ref)
    _to_token_lanes(s2_s, e2_ref)
    _to_token_lanes(n1_s, n1_ref)
    _to_token_lanes(s1_s, c1_ref)


def _route(hn, w_q, keys):
    T = hn.shape[0]
    tm = ROUTE_TM
    out_sds = lambda dt: jax.ShapeDtypeStruct((PEER_HEADS, T // LANES, PEER_NKEYS, LANES), dt)
    out_spec = pl.BlockSpec((None, SUBLANES, PEER_NKEYS, LANES), lambda i, h: (h, i, 0, 0))
    km = pltpu.VMEM((PEER_NKEYS * SUBLANES, LANES), F32)
    return pl.pallas_call(
        _route_body,
        out_shape=(out_sds(F32),) * 4,
        grid=(T // tm, PEER_HEADS),
        in_specs=[pl.BlockSpec((tm, D_MODEL), lambda i, h: (i, 0)),
                  pl.BlockSpec((D_MODEL, 2 * PEER_HALF), lambda i, h: (0, h)),
                  pl.BlockSpec((None, 2, PEER_NKEYS, PEER_HALF), lambda i, h: (h, 0, 0, 0))],
        out_specs=(out_spec,) * 4,
        scratch_shapes=[km] * 4,
        compiler_params=_cparams(2),
        name="route",
    )(hn, w_q, keys)


PEER_TE = 512
PEER_QROWS = 32
INV_SQRT2 = 0.7071067811865476


def _peer_tile(a_prev, a_next, jprev, live, hn_ref, u_ref, vt_ref, r2_ref, e2_ref, n1_ref, c1_ref, o_ref, g_scr):
    n_il = PEER_TE // PEER_NKEYS
    reps = PEER_QROWS // SUBLANES
    n_qd = PEER_NKEYS // PEER_QROWS
    half = 0.5 * live
    a_next[...] = lax.dot_general(u_ref[...], hn_ref[...], NT, preferred_element_type=F32)
    for il0 in range(0, n_il, 2):
        ils = (il0, il0 + 1)
        for cb in range(g_scr.shape[1] // LANES):
            lanes = slice(cb * LANES, (cb + 1) * LANES)
            w = {(il, qd): None for il in ils for qd in range(n_qd)}
            for h in range(PEER_HEADS):
                row = {}
                for il in ils:
                    key = pl.ds(jprev * n_il + il, SUBLANES, stride=0)
                    row[il] = (jnp.concatenate([n1_ref[h, cb, key, :]] * reps, axis=0),
                               jnp.concatenate([c1_ref[h, cb, key, :]] * reps, axis=0))
                for qd in range(n_qd):
                    rows = slice(qd * PEER_QROWS, (qd + 1) * PEER_QROWS)
                    r2 = r2_ref[h, cb, rows, :]
                    e2 = e2_ref[h, cb, rows, :]
                    for il in ils:
                        t = jnp.where(r2 < row[il][0], e2, 0.0) * row[il][1]
                        w[il, qd] = t if w[il, qd] is None else w[il, qd] + t
            for il in ils:
                for qd in range(n_qd):
                    er = slice(il * PEER_NKEYS + qd * PEER_QROWS, il * PEER_NKEYS + (qd + 1) * PEER_QROWS)
                    a = a_prev[er, lanes]
                    act = half * a * (1.0 + lax.erf(a * INV_SQRT2))
                    g_scr[er, lanes] = (act * w[il, qd]).astype(BF16)
    o_ref[...] += jnp.dot(vt_ref[...], g_scr[...], preferred_element_type=F32)


def _peer_body(hn_ref, u_ref, vt_ref, r2_ref, e2_ref, n1_ref, c1_ref, o_ref, a0, a1, g_scr):
    i = pl.program_id(0)
    j = pl.program_id(1)

    @pl.when(jnp.logical_and(i == 0, j == 0))
    def _():
        a1[...] = jnp.zeros_like(a1)

    @pl.when(j == 0)
    def _():
        o_ref[...] = jnp.zeros_like(o_ref)

    live = jnp.where(j > 0, 1.0, 0.0).astype(F32)
    jprev = jnp.maximum(j - 1, 0)
    tile = functools.partial(_peer_tile, jprev=jprev, live=live, hn_ref=hn_ref, u_ref=u_ref, vt_ref=vt_ref,
                             r2_ref=r2_ref, e2_ref=e2_ref, n1_ref=n1_ref, c1_ref=c1_ref, o_ref=o_ref, g_scr=g_scr)

    @pl.when(j % 2 == 0)
    def _():
        tile(a1, a0)

    @pl.when(j % 2 == 1)
    def _():
        tile(a0, a1)


def _peer(hn, u, vt, r2, e2, n1, c1, tm=512):
    T = hn.shape[0]
    te = PEER_TE
    nj = PEER_EXPERTS // te
    rspec = pl.BlockSpec((PEER_HEADS, tm // LANES, PEER_NKEYS, LANES), lambda i, j: (0, i, 0, 0))
    return pl.pallas_call(
        _peer_body,
        out_shape=jax.ShapeDtypeStruct((D_MODEL, T), F32),
        grid=(T // tm, nj + 1),
        in_specs=[pl.BlockSpec((tm, D_MODEL), lambda i, j: (i, 0)),
                  pl.BlockSpec((te, D_MODEL), lambda i, j: (jnp.minimum(j, nj - 1), 0)),
                  pl.BlockSpec((None, D_MODEL, te), lambda i, j: (jnp.maximum(j - 1, 0), 0, 0)),
                  rspec, rspec, rspec, rspec],
        out_specs=pl.BlockSpec((D_MODEL, tm), lambda i, j: (0, i)),
        scratch_shapes=[pltpu.VMEM((te, tm), F32), pltpu.VMEM((te, tm), F32), pltpu.VMEM((te, tm), BF16)],
        compiler_params=_cparams(2),
        name="peer",
    )(hn, u, vt, r2, e2, n1, c1)


def _final_body(h_ref, pt_ref, g_ref, o_ref):
    o_ref[...] = _rms(h_ref[...] + pt_ref[...].T, g_ref[...])


def _final(h, peer_t, g, tm=512):
    T = h.shape[0]
    return pl.pallas_call(
        _final_body,
        out_shape=jax.ShapeDtypeStruct((T, D_MODEL), F32),
        grid=(T // tm,),
        in_specs=[pl.BlockSpec((tm, D_MODEL), lambda i: (i, 0)),
                  pl.BlockSpec((D_MODEL, tm), lambda i: (0, i)),
                  pl.BlockSpec((1, D_MODEL), lambda i: (0, 0))],
        out_specs=pl.BlockSpec((tm, D_MODEL), lambda i: (i, 0)),
        compiler_params=_cparams(1),
        name="final",
    )(h, peer_t, g)


def _layer(x, norm1_g, w_in, up_f, bias_f, up_b, bias_b, gla_norm_g, att_norm_g, w_out, norm2_g,
           w_q, sub_keys, peer_u, peer_v):
    B, S, D = x.shape
    T = B * S
    x2 = x.reshape(T, D)
    gd0 = 2 * GLA_QK + 2 * GLA_V
    w_main = jnp.concatenate([w_in[:, :gd0], w_in[:, gd0 + GD_W:]], axis=1).astype(BF16)
    w_gd = w_in[:, gd0:gd0 + GD_W].astype(BF16)
    main, gd = _inproj(x2, norm1_g.reshape(1, D), w_main, w_gd)
    main3 = main.reshape(B, S, MAIN_W)
    o_f, o_b = _gla(main3, gd.reshape(B, S, GD_W), up_f, bias_f.reshape(1, GLA_QK),
                    up_b, bias_b.reshape(1, GLA_QK))
    slopes = jnp.asarray((2.0 ** (-8.0 * np.arange(1, ATT_HEADS + 1) / ATT_HEADS)).astype(np.float32))
    att = _attn(main3, slopes)
    h, hn = _mix(o_f.reshape(T, GLA_V), o_b.reshape(T, GLA_V), main, att.reshape(T, ATT_W), x2,
                 gla_norm_g.reshape(1, GLA_DV), att_norm_g.reshape(1, ATT_W), w_out.astype(BF16),
                 norm2_g.reshape(1, D))
    r2, e2, n1, c1 = _route(hn, w_q.astype(BF16), sub_keys.astype(BF16))
    vt = peer_v.astype(BF16).reshape(PEER_EXPERTS // PEER_TE, PEER_TE, D).transpose(0, 2, 1)
    peer_t = _peer(hn, peer_u.astype(BF16), vt, r2, e2, n1, c1)
    return h, peer_t


def kernel(x, norm1_g, w_in, gla_gate_up_f, gla_gate_bias_f, gla_gate_up_b, gla_gate_bias_b, gla_norm_g,
           att_norm_g, w_out, norm2_g, peer_w_q, peer_sub_keys, peer_u, peer_v, final_norm_g):
    B, S, D = x.shape
    assert norm1_g.shape[0] == 1, "single trunk layer: the final norm is fused with the layer's last residual"
    h, peer_t = _layer(x, norm1_g[0], w_in[0], gla_gate_up_f[0], gla_gate_bias_f[0], gla_gate_up_b[0],
                       gla_gate_bias_b[0], gla_norm_g[0], att_norm_g[0], w_out[0], norm2_g[0],
                       peer_w_q[0], peer_sub_keys[0], peer_u[0], peer_v[0])
    return _final(h, peer_t, final_norm_g.reshape(1, D)).reshape(B, S, D)
```

```python
import functools

import numpy as np
import jax
import jax.numpy as jnp
from jax import lax
from jax.experimental import pallas as pl
from jax.experimental.pallas import tpu as pltpu

F32 = jnp.float32
BF16 = jnp.bfloat16
I32 = jnp.int32

D_MODEL = 2048
GLA_HEADS = 4
GLA_DK = 128
GLA_DV = 256
GLA_RANK = 16
GLA_TAU = 16.0
GLA_CHUNK = 64
GLA_QK = GLA_HEADS * GLA_DK
GLA_V = GLA_HEADS * GLA_DV
ATT_HEADS = 16
ATT_HD = 64
ATT_W = ATT_HEADS * ATT_HD
ATT_DILATIONS = (1, 4, 16)
ATT_RADIUS = 64
PEER_NKEYS = 128
PEER_HEADS = 8
PEER_TOPK = 16
PEER_EXPERTS = PEER_NKEYS * PEER_NKEYS
PEER_HALF = 128
EPS = 1e-6
NEG = -1e30
MAIN_W = 2 * GLA_QK + 2 * GLA_V + 3 * ATT_W
GD_W = 2 * GLA_RANK

LANES = 128
SUBLANES = 8
VMEM_LIMIT = 56 * 1024 * 1024

NT = (((1,), (1,)), ((), ()))
TN = (((0,), (0,)), ((), ()))


def _cparams(n_axes, flags=None):
    return pltpu.CompilerParams(dimension_semantics=("arbitrary",) * n_axes,
                                vmem_limit_bytes=VMEM_LIMIT, flags=flags)


def _rms(x, g):
    ms = jnp.mean(x * x, axis=-1, keepdims=True)
    return x * lax.rsqrt(ms + EPS) * g


def _inproj_body(x_ref, g_ref, w_ref, wgd_ref, o_ref, gd_ref, xn_ref):
    @pl.when(pl.program_id(1) == 0)
    def _():
        xn = _rms(x_ref[...], g_ref[...]).astype(BF16)
        xn_ref[...] = xn
        gd_ref[...] = jnp.dot(xn, wgd_ref[...], preferred_element_type=F32)

    o_ref[...] = jnp.dot(xn_ref[...], w_ref[...], preferred_element_type=F32)


def _inproj(x2, g, w_main, w_gd, tm=1024, tn=1536):
    T = x2.shape[0]
    return pl.pallas_call(
        _inproj_body,
        out_shape=(jax.ShapeDtypeStruct((T, MAIN_W), F32), jax.ShapeDtypeStruct((T, GD_W), F32)),
        grid=(T // tm, MAIN_W // tn),
        in_specs=[pl.BlockSpec((tm, D_MODEL), lambda i, j: (i, 0)),
                  pl.BlockSpec((1, D_MODEL), lambda i, j: (0, 0)),
                  pl.BlockSpec((D_MODEL, tn), lambda i, j: (0, j)),
                  pl.BlockSpec((D_MODEL, GD_W), lambda i, j: (0, 0))],
        out_specs=(pl.BlockSpec((tm, tn), lambda i, j: (i, j)),
                   pl.BlockSpec((tm, GD_W), lambda i, j: (i, 0))),
        scratch_shapes=[pltpu.VMEM((tm, D_MODEL), BF16)],
        compiler_params=_cparams(2),
        name="inproj",
    )(x2, g, w_main, w_gd)


def _log_gate(gd, up_ref, bias_ref):
    z = jnp.dot(gd.astype(BF16), up_ref[...].astype(BF16), preferred_element_type=F32) + bias_ref[...]
    return (jnp.minimum(z, 0.0) - jnp.log1p(jnp.exp(-jnp.abs(z)))) * (1.0 / GLA_TAU)


def _gla_chunk(q_ref, k_ref, v_ref, la_ref, st_ref, o_ref, r0, tri_mask, tot_row):
    C = GLA_CHUNK
    sl = pl.ds(pl.multiple_of(r0, C), C)
    la = la_ref[sl, :]
    hi = la.astype(BF16)
    lo = (la - hi.astype(F32)).astype(BF16)
    tri = jnp.where(tri_mask, 1.0, 0.0).astype(BF16)
    b = (jnp.dot(tri, hi, preferred_element_type=F32) + jnp.dot(tri, lo, preferred_element_type=F32))
    btot = b[tot_row:tot_row + 1, :]
    q = q_ref[sl, :]
    k = k_ref[sl, :]
    v = v_ref[sl, :].astype(BF16)
    qd = (q * (jnp.exp(b) * (GLA_DK ** -0.5))).astype(BF16)
    kd = (k * jnp.exp(-b)).astype(BF16)
    kr = (k * jnp.exp(btot - b)).astype(BF16)
    att = lax.dot_general(qd, kd, NT, preferred_element_type=F32)
    att = jnp.where(tri_mask, att, 0.0).astype(BF16)
    st = st_ref[...]
    o = (jnp.dot(att, v, preferred_element_type=F32)
         + lax.dot_general(qd, st.astype(BF16), NT, preferred_element_type=F32))
    o_ref[sl, :] = o
    st_ref[...] = jnp.exp(btot) * st + lax.dot_general(v, kr, TN, preferred_element_type=F32)


def _gla_body(qf, kf, vf, gdf, qb, kb, vb, gdb, upf, bsf, upb, bsb, of_ref, ob_ref,
              stf, stb, laf, lab, *, nchunk):
    @pl.when(pl.program_id(2) == 0)
    def _():
        stf[...] = jnp.zeros_like(stf)
        stb[...] = jnp.zeros_like(stb)

    laf[...] = _log_gate(gdf[:, 0:GLA_RANK], upf, bsf)
    lab[...] = _log_gate(gdb[:, GLA_RANK:2 * GLA_RANK], upb, bsb)
    C = GLA_CHUNK
    row = lax.broadcasted_iota(I32, (C, C), 0)
    col = lax.broadcasted_iota(I32, (C, C), 1)
    lower = row >= col
    upper = col >= row

    def step(c, carry):
        _gla_chunk(qf, kf, vf, laf, stf, of_ref, c * C, lower, C - 1)
        _gla_chunk(qb, kb, vb, lab, stb, ob_ref, (nchunk - 1 - c) * C, upper, 0)
        return carry

    lax.fori_loop(0, nchunk, step, 0, unroll=16)


def _gla(main3, gd3, up_f, bias_f, up_b, bias_b, lb=1024):
    B, S, _ = main3.shape
    nb = S // lb
    kq = GLA_QK // GLA_DK
    kv = (2 * GLA_QK) // GLA_DV
    fwd = lambda b, h, i: (b, i, h)
    blk = lambda c0: (lambda b, h, i: (b, i, c0 + h))
    rblk = lambda c0: (lambda b, h, i: (b, nb - 1 - i, c0 + h))
    sq = lambda m: pl.BlockSpec((None, lb, GLA_DK), m)
    sv = lambda m: pl.BlockSpec((None, lb, GLA_DV), m)
    sg = lambda m: pl.BlockSpec((None, lb, GD_W), m)
    up_spec = pl.BlockSpec((GLA_RANK, GLA_DK), lambda b, h, i: (0, h))
    bias_spec = pl.BlockSpec((1, GLA_DK), lambda b, h, i: (0, h))
    out_sds = jax.ShapeDtypeStruct((B, S, GLA_V), F32)
    return pl.pallas_call(
        functools.partial(_gla_body, nchunk=lb // GLA_CHUNK),
        out_shape=(out_sds, out_sds),
        grid=(B, GLA_HEADS, nb),
        in_specs=[sq(blk(0)), sq(blk(kq)), sv(blk(kv)), sg(lambda b, h, i: (b, i, 0)),
                  sq(rblk(0)), sq(rblk(kq)), sv(rblk(kv)), sg(lambda b, h, i: (b, nb - 1 - i, 0)),
                  up_spec, bias_spec, up_spec, bias_spec],
        out_specs=(pl.BlockSpec((None, lb, GLA_DV), fwd),
                   pl.BlockSpec((None, lb, GLA_DV), lambda b, h, i: (b, nb - 1 - i, h))),
        scratch_shapes=[pltpu.VMEM((GLA_DV, GLA_DK), F32), pltpu.VMEM((GLA_DV, GLA_DK), F32),
                        pltpu.VMEM((lb, GLA_DK), F32), pltpu.VMEM((lb, GLA_DK), F32)],
        compiler_params=_cparams(3),
        name="gla",
    )(main3, main3, main3, gd3, main3, main3, main3, gd3, up_f, bias_f, up_b, bias_b)


ATT_TQ = 128
ATT_WIN = 256


ATT_GROUP = 32


ATT_FACTOR = 4
assert all(b == a * ATT_FACTOR for a, b in zip(ATT_DILATIONS[:-1], ATT_DILATIONS[1:])) and ATT_DILATIONS[0] == 1


def _split_classes(src, dst, d, n):
    f = ATT_FACTOR
    sub = n // (d * f)
    for r in range(d):
        for a in range(f):
            dst[pl.ds((d * a + r) * sub, sub), :] = src[pl.ds(r * (n // d) + a, sub, stride=f), :]


def _merge_classes(src, dst, d, n):
    f = ATT_FACTOR
    sub = n // (d * f)
    for r in range(d):
        for a in range(f):
            dst[pl.ds(r * (n // d) + a, sub, stride=f), :] = src[pl.ds((d * a + r) * sub, sub), :]


def _attn_body(slopes_ref, q_ref, k_ref, v_ref, o_ref, *scr, seq):
    pair = pl.program_id(1)
    lane = lax.broadcasted_iota(I32, (1, LANES), 1)
    head0 = lane < ATT_HD
    ii = lax.broadcasted_iota(I32, (ATT_TQ, ATT_WIN), 0)
    jj = lax.broadcasted_iota(I32, (ATT_TQ, ATT_WIN), 1)
    slope = (slopes_ref[2 * pair], slopes_ref[2 * pair + 1])
    npat = len(ATT_DILATIONS)
    copies = [(q_ref, k_ref, v_ref)] + [scr[3 * i:3 * i + 3] for i in range(npat - 1)]
    stats = [scr[3 * (npat - 1):3 * npat], scr[3 * npat:3 * (npat + 1)]]
    bias_s = scr[3 * (npat + 1)]
    for pi in range(1, npat):
        for x in range(3):
            _split_classes(copies[pi - 1][x], copies[pi][x], ATT_DILATIONS[pi - 1], seq)

    for step, pi in enumerate(reversed(range(npat))):
        d = ATT_DILATIONS[pi]
        L = seq // d
        nblk = L // ATT_TQ
        first = step == 0
        q_ref, k_ref, v_ref = copies[pi]
        if not first:
            for x in range(3):
                _merge_classes(stats[(step - 1) % 2][x], stats[step % 2][x], d, seq)
        m_s, l_s, acc_s = stats[step % 2]
        for case in range(3):
            ad = jnp.abs(jj - ii - case * ATT_RADIUS)
            for hh in range(2):
                bias_s[case, hh] = jnp.where(ad <= ATT_RADIUS, -slope[hh] * (ad * d).astype(F32), NEG)

        def group(gi, carry, L=L, nblk=nblk, first=first, q_ref=q_ref, k_ref=k_ref, v_ref=v_ref,
                  m_s=m_s, l_s=l_s, acc_s=acc_s):
            loaded = []
            for u in range(ATT_GROUP):
                t = gi * ATT_GROUP + u
                base = (t // nblk) * L
                q0 = (t % nblk) * ATT_TQ
                start = jnp.clip(q0 - ATT_RADIUS, 0, L - ATT_WIN)
                case = (q0 - start) // ATT_RADIUS
                qsl = pl.ds(pl.multiple_of(base + q0, ATT_TQ), ATT_TQ)
                ksl = pl.ds(pl.multiple_of(base + start, ATT_RADIUS), ATT_WIN)
                q = q_ref[qsl, :] * (ATT_HD ** -0.5)
                k = k_ref[ksl, :].astype(BF16)
                v = v_ref[ksl, :].astype(BF16)
                run = None if first else (m_s[qsl, :], l_s[qsl, :], acc_s[qsl, :])
                loaded.append((qsl, case, q, k, v, run))
            results = []
            for qsl, case, q, k, v, run in loaded:
                m_new, lsum, pv = [], [], []
                for hh in range(2):
                    hm = head0 if hh == 0 else jnp.logical_not(head0)
                    qh = jnp.where(hm, q, 0.0).astype(BF16)
                    s = lax.dot_general(qh, k, NT, preferred_element_type=F32) + bias_s[case, hh]
                    mh = jnp.max(s, axis=-1, keepdims=True)
                    p = jnp.exp(s - mh)
                    m_new.append(mh)
                    lsum.append(jnp.sum(p, axis=-1, keepdims=True))
                    pv.append(jnp.dot(p.astype(BF16), v, preferred_element_type=F32))
                m_full = jnp.where(head0, m_new[0], m_new[1])
                l_blk = jnp.where(head0, lsum[0], lsum[1])
                pv_blk = jnp.where(head0, pv[0], pv[1])
                if not first:
                    m_blk = m_full
                    m_full = jnp.maximum(run[0], m_blk)
                    a_run = jnp.exp(run[0] - m_full)
                    a_blk = jnp.exp(m_blk - m_full)
                    l_blk = a_run * run[1] + a_blk * l_blk
                    pv_blk = a_run * run[2] + a_blk * pv_blk
                results.append((qsl, m_full, l_blk, pv_blk))
            for qsl, m_full, l_blk, pv_blk in results:
                m_s[qsl, :] = m_full
                l_s[qsl, :] = l_blk
                acc_s[qsl, :] = pv_blk
            return carry

        lax.fori_loop(0, d * nblk // ATT_GROUP, group, 0)

    o_ref[...] = acc_s[...] / l_s[...]


def _attn(main3, slopes):
    B, S, _ = main3.shape
    c0 = (2 * GLA_QK + 2 * GLA_V) // LANES
    npair = ATT_HEADS // 2
    spec = lambda off: pl.BlockSpec((None, S, LANES), lambda b, p: (b, 0, c0 + off + p))
    return pl.pallas_call(
        functools.partial(_attn_body, seq=S),
        out_shape=jax.ShapeDtypeStruct((B, S, ATT_W), F32),
        grid=(B, npair),
        in_specs=[pl.BlockSpec(memory_space=pltpu.SMEM), spec(0), spec(npair), spec(2 * npair)],
        out_specs=pl.BlockSpec((None, S, LANES), lambda b, p: (b, 0, p)),
        scratch_shapes=([pltpu.VMEM((S, LANES), F32)] * (3 * (len(ATT_DILATIONS) + 1))
                        + [pltpu.VMEM((3, 2, ATT_TQ, ATT_WIN), F32)]),
        compiler_params=_cparams(2),
        name="attn",
    )(slopes, main3, main3, main3)


def _mix_body(of_ref, ob_ref, gr_ref, att_ref, x_ref, gg_ref, ag_ref, wo_ref, n2_ref, h_ref, hn_ref):
    o = of_ref[...] + ob_ref[...]
    parts = [_rms(o[:, h * GLA_DV:(h + 1) * GLA_DV], gg_ref[...]) for h in range(GLA_HEADS)]
    on = jnp.concatenate(parts, axis=-1)
    gr = gr_ref[...]
    gla = on * (gr / (1.0 + jnp.exp(-gr)))
    an = _rms(att_ref[...], ag_ref[...])
    y = (jnp.dot(gla.astype(BF16), wo_ref[0:GLA_V, :], preferred_element_type=F32)
         + jnp.dot(an.astype(BF16), wo_ref[GLA_V:GLA_V + ATT_W, :], preferred_element_type=F32))
    h = x_ref[...] + y
    h_ref[...] = h
    hn_ref[...] = _rms(h, n2_ref[...]).astype(BF16)


def _mix(o_f, o_b, main2, att2, x2, gg, ag, w_out, n2g, tm=512):
    T = x2.shape[0]
    gr_blk = (2 * GLA_QK + GLA_V) // GLA_V
    row = lambda i: (i, 0)
    const = lambda i: (0, 0)
    return pl.pallas_call(
        _mix_body,
        out_shape=(jax.ShapeDtypeStruct((T, D_MODEL), F32), jax.ShapeDtypeStruct((T, D_MODEL), BF16)),
        grid=(T // tm,),
        in_specs=[pl.BlockSpec((tm, GLA_V), row), pl.BlockSpec((tm, GLA_V), row),
                  pl.BlockSpec((tm, GLA_V), lambda i: (i, gr_blk)),
                  pl.BlockSpec((tm, ATT_W), row), pl.BlockSpec((tm, D_MODEL), row),
                  pl.BlockSpec((1, GLA_DV), const), pl.BlockSpec((1, ATT_W), const),
                  pl.BlockSpec((D_MODEL, D_MODEL), const), pl.BlockSpec((1, D_MODEL), const)],
        out_specs=(pl.BlockSpec((tm, D_MODEL), row), pl.BlockSpec((tm, D_MODEL), row)),
        compiler_params=_cparams(1),
        name="mix",
    )(o_f, o_b, main2, att2, x2, gg, ag, w_out, n2g)


ROUTE_TM = SUBLANES * LANES
_CELLS = [(a, b) for a in range(PEER_TOPK) for b in range(PEER_TOPK) if (a + 1) * (b + 1) <= PEER_TOPK]


def _key_rows(i):
    return pl.ds(i * SUBLANES, SUBLANES)


def _before(a, b):
    (va, ia), (vb, ib) = a, b
    if isinstance(ia, float) and isinstance(ib, float):
        return (va >= vb) if ia < ib else (va > vb)
    return jnp.logical_or(va > vb, jnp.logical_and(va == vb, ia < ib))


def _pick(c, a, b):
    return tuple(jnp.where(c, x, y) for x, y in zip(a, b))


def _ordered_pair(a, b):
    c = _before(a, b)
    return _pick(c, a, b), _pick(c, b, a)


def _bitonic_merge(xs):
    n = len(xs)
    if n == 1:
        return list(xs)
    h = n // 2
    xs = list(xs)
    for k in range(h):
        xs[k], xs[k + h] = _ordered_pair(xs[k], xs[k + h])
    return _bitonic_merge(xs[:h]) + _bitonic_merge(xs[h:])


def _bitonic_sort(xs):
    n = len(xs)
    if n == 1:
        return list(xs)
    return _bitonic_merge(_bitonic_sort(xs[:n // 2]) + _bitonic_sort(xs[n // 2:])[::-1])


def _top_keys(s_ref):
    k = PEER_TOPK
    best = None
    for g in range(PEER_NKEYS // k):
        grp = _bitonic_sort([(s_ref[_key_rows(g * k + a), :], float(g * k + a)) for a in range(k)])
        grp = [(v, jnp.full((SUBLANES, LANES), i, F32) if isinstance(i, float) else i) for v, i in grp]
        if best is None:
            best = grp
        else:
            best = _bitonic_merge([_pick(_before(best[j], grp[k - 1 - j]), best[j], grp[k - 1 - j])
                                   for j in range(k)])
    return [v for v, _ in best], [i for _, i in best]


def _to_token_lanes(src_ref, dst_ref):
    for g in range(SUBLANES):
        dst_ref[g] = src_ref[pl.ds(g, PEER_NKEYS, stride=SUBLANES), :].astype(dst_ref.dtype)


def _route_body(hn_ref, wq_ref, keys_ref, r2_ref, e2_ref, n1_ref, c1_ref, s1_s, s2_s, rk2_s, n1_s):
    q = jnp.dot(hn_ref[...], wq_ref[...], preferred_element_type=F32)
    for c, s_s in enumerate((s1_s, s2_s)):
        qc = q[:, c * PEER_HALF:(c + 1) * PEER_HALF].astype(BF16)
        st = lax.dot_general(keys_ref[c], qc, NT, preferred_element_type=F32)
        for g in range(SUBLANES):
            s_s[pl.ds(g, PEER_NKEYS, stride=SUBLANES), :] = st[:, g * LANES:(g + 1) * LANES]
    sc1, si1 = _top_keys(s1_s)
    sc2, si2 = _top_keys(s2_s)
    e1 = [jnp.exp(sc1[k] - sc1[0]) for k in range(PEER_TOPK)]
    e2 = [jnp.exp(sc2[k] - sc2[0]) for k in range(PEER_TOPK)]

    cand = [sc1[a] + sc2[b] for (a, b) in _CELLS]
    nc = len(_CELLS)
    before = [jnp.zeros((SUBLANES, LANES), F32) for _ in range(nc)]
    for x in range(nc):
        for y in range(x + 1, nc):
            ax, bx = _CELLS[x]
            ay, by = _CELLS[y]
            if ax <= ay and bx <= by:
                before[y] = before[y] + 1.0
                continue
            bt = jnp.where(cand[x] >= cand[y], 1.0, 0.0)
            before[y] = before[y] + bt
            before[x] = before[x] + (1.0 - bt)
    zero = jnp.zeros((SUBLANES, LANES), F32)
    ncol = [zero] * PEER_TOPK
    zsum = zero
    for x, (a, b) in enumerate(_CELLS):
        sel = jnp.where(before[x] < float(PEER_TOPK), 1.0, 0.0)
        ncol[a] = ncol[a] + sel
        zsum = zsum + sel * (e1[a] * e2[b])
    inv_z = 1.0 / zsum

    def dense(i, carry):
        rows = _key_rows(i)
        key = lax.convert_element_type(i, F32)
        n1 = zero
        rk2 = jnp.full((SUBLANES, LANES), float(PEER_TOPK), F32)
        for k in range(PEER_TOPK):
            n1 = jnp.where(si1[k] == key, ncol[k], n1)
            rk2 = jnp.where(si2[k] == key, float(k), rk2)
        n1_s[rows, :] = n1
        rk2_s[rows, :] = rk2
        s1_s[rows, :] = jnp.exp(s1_s[rows, :] - sc1[0]) * inv_z
        s2_s[rows, :] = jnp.exp(s2_s[rows, :] - sc2[0])
        return carry

    lax.fori_loop(0, PEER_NKEYS, dense, 0, unroll=4)
    _to_token_lanes(rk2_s, r2_ref)
    _to_token_lanes(s2_s, e2_ref)
    _to_token_lanes(n1_s, n1_ref)
    _to_token_lanes(s1_s, c1_ref)


def _route(hn, w_q, keys):
    T = hn.shape[0]
    tm = ROUTE_TM
    out_sds = lambda dt: jax.ShapeDtypeStruct((PEER_HEADS, T // LANES, PEER_NKEYS, LANES), dt)
    out_spec = pl.BlockSpec((None, SUBLANES, PEER_NKEYS, LANES), lambda i, h: (h, i, 0, 0))
    km = pltpu.VMEM((PEER_NKEYS * SUBLANES, LANES), F32)
    return pl.pallas_call(
        _route_body,
        out_shape=(out_sds(F32),) * 4,
        grid=(T // tm, PEER_HEADS),
        in_specs=[pl.BlockSpec((tm, D_MODEL), lambda i, h: (i, 0)),
                  pl.BlockSpec((D_MODEL, 2 * PEER_HALF), lambda i, h: (0, h)),
                  pl.BlockSpec((None, 2, PEER_NKEYS, PEER_HALF), lambda i, h: (h, 0, 0, 0))],
        out_specs=(out_spec,) * 4,
        scratch_shapes=[km] * 4,
        compiler_params=_cparams(2),
        name="route",
    )(hn, w_q, keys)


PEER_TE = 512
PEER_QROWS = 32
INV_SQRT2 = 0.7071067811865476


def _peer_tile(a_prev, a_next, jprev, live, hn_ref, u_ref, vt_ref, r2_ref, e2_ref, n1_ref, c1_ref, o_ref, g_scr):
    n_il = PEER_TE // PEER_NKEYS
    reps = PEER_QROWS // SUBLANES
    n_qd = PEER_NKEYS // PEER_QROWS
    half = 0.5 * live
    a_next[...] = lax.dot_general(u_ref[...], hn_ref[...], NT, preferred_element_type=F32)
    for il0 in range(0, n_il, 2):
        ils = (il0, il0 + 1)
        for cb in range(g_scr.shape[1] // LANES):
            lanes = slice(cb * LANES, (cb + 1) * LANES)
            w = {(il, qd): None for il in ils for qd in range(n_qd)}
            for h in range(PEER_HEADS):
                row = {}
                for il in ils:
                    key = pl.ds(jprev * n_il + il, SUBLANES, stride=0)
                    row[il] = (jnp.concatenate([n1_ref[h, cb, key, :]] * reps, axis=0),
                               jnp.concatenate([c1_ref[h, cb, key, :]] * reps, axis=0))
                for qd in range(n_qd):
                    rows = slice(qd * PEER_QROWS, (qd + 1) * PEER_QROWS)
                    r2 = r2_ref[h, cb, rows, :]
                    e2 = e2_ref[h, cb, rows, :]
                    for il in ils:
                        t = jnp.where(r2 < row[il][0], e2, 0.0) * row[il][1]
                        w[il, qd] = t if w[il, qd] is None else w[il, qd] + t
            for il in ils:
                for qd in range(n_qd):
                    er = slice(il * PEER_NKEYS + qd * PEER_QROWS, il * PEER_NKEYS + (qd + 1) * PEER_QROWS)
                    a = a_prev[er, lanes]
                    act = half * a * (1.0 + lax.erf(a * INV_SQRT2))
                    g_scr[er, lanes] = (act * w[il, qd]).astype(BF16)
    o_ref[...] += jnp.dot(vt_ref[...], g_scr[...], preferred_element_type=F32)


def _peer_body(hn_ref, u_ref, vt_ref, r2_ref, e2_ref, n1_ref, c1_ref, o_ref, a0, a1, g_scr):
    i = pl.program_id(0)
    j = pl.program_id(1)

    @pl.when(jnp.logical_and(i == 0, j == 0))
    def _():
        a1[...] = jnp.zeros_like(a1)

    @pl.when(j == 0)
    def _():
        o_ref[...] = jnp.zeros_like(o_ref)

    live = jnp.where(j > 0, 1.0, 0.0).astype(F32)
    jprev = jnp.maximum(j - 1, 0)
    tile = functools.partial(_peer_tile, jprev=jprev, live=live, hn_ref=hn_ref, u_ref=u_ref, vt_ref=vt_ref,
                             r2_ref=r2_ref, e2_ref=e2_ref, n1_ref=n1_ref, c1_ref=c1_ref, o_ref=o_ref, g_scr=g_scr)

    @pl.when(j % 2 == 0)
    def _():
        tile(a1, a0)

    @pl.when(j % 2 == 1)
    def _():
        tile(a0, a1)


def _peer(hn, u, vt, r2, e2, n1, c1, tm=512):
    T = hn.shape[0]
    te = PEER_TE
    nj = PEER_EXPERTS // te
    rspec = pl.BlockSpec((PEER_HEADS, tm // LANES, PEER_NKEYS, LANES), lambda i, j: (0, i, 0, 0))
    return pl.pallas_call(
        _peer_body,
        out_shape=jax.ShapeDtypeStruct((D_MODEL, T), F32),
        grid=(T // tm, nj + 1),
        in_specs=[pl.BlockSpec((tm, D_MODEL), lambda i, j: (i, 0)),
                  pl.BlockSpec((te, D_MODEL), lambda i, j: (jnp.minimum(j, nj - 1), 0)),
                  pl.BlockSpec((None, D_MODEL, te), lambda i, j: (jnp.maximum(j - 1, 0), 0, 0)),
                  rspec, rspec, rspec, rspec],
        out_specs=pl.BlockSpec((D_MODEL, tm), lambda i, j: (0, i)),
        scratch_shapes=[pltpu.VMEM((te, tm), F32), pltpu.VMEM((te, tm), F32), pltpu.VMEM((te, tm), BF16)],
        compiler_params=_cparams(2),
        name="peer",
    )(hn, u, vt, r2, e2, n1, c1)


def _final_body(h_ref, pt_ref, g_ref, o_ref):
    o_ref[...] = _rms(h_ref[...] + pt_ref[...].T, g_ref[...])


def _final(h, peer_t, g, tm=512):
    T = h.shape[0]
    return pl.pallas_call(
        _final_body,
        out_shape=jax.ShapeDtypeStruct((T, D_MODEL), F32),
        grid=(T // tm,),
        in_specs=[pl.BlockSpec((tm, D_MODEL), lambda i: (i, 0)),
                  pl.BlockSpec((D_MODEL, tm), lambda i: (0, i)),
                  pl.BlockSpec((1, D_MODEL), lambda i: (0, 0))],
        out_specs=pl.BlockSpec((tm, D_MODEL), lambda i: (i, 0)),
        compiler_params=_cparams(1),
        name="final",
    )(h, peer_t, g)


def _layer(x, norm1_g, w_in, up_f, bias_f, up_b, bias_b, gla_norm_g, att_norm_g, w_out, norm2_g,
           w_q, sub_keys, peer_u, peer_v):
    B, S, D = x.shape
    T = B * S
    x2 = x.reshape(T, D)
    gd0 = 2 * GLA_QK + 2 * GLA_V
    w_main = jnp.concatenate([w_in[:, :gd0], w_in[:, gd0 + GD_W:]], axis=1).astype(BF16)
    w_gd = w_in[:, gd0:gd0 + GD_W].astype(BF16)
    main, gd = _inproj(x2, norm1_g.reshape(1, D), w_main, w_gd)
    main3 = main.reshape(B, S, MAIN_W)
    o_f, o_b = _gla(main3, gd.reshape(B, S, GD_W), up_f, bias_f.reshape(1, GLA_QK),
                    up_b, bias_b.reshape(1, GLA_QK))
    slopes = jnp.asarray((2.0 ** (-8.0 * np.arange(1, ATT_HEADS + 1) / ATT_HEADS)).astype(np.float32))
    att = _attn(main3, slopes)
    h, hn = _mix(o_f.reshape(T, GLA_V), o_b.reshape(T, GLA_V), main, att.reshape(T, ATT_W), x2,
                 gla_norm_g.reshape(1, GLA_DV), att_norm_g.reshape(1, ATT_W), w_out.astype(BF16),
                 norm2_g.reshape(1, D))
    r2, e2, n1, c1 = _route(hn, w_q.astype(BF16), sub_keys.astype(BF16))
    vt = peer_v.astype(BF16).reshape(PEER_EXPERTS // PEER_TE, PEER_TE, D).transpose(0, 2, 1)
    peer_t = _peer(hn, peer_u.astype(BF16), vt, r2, e2, n1, c1)
    return h, peer_t


def kernel(x, norm1_g, w_in, gla_gate_up_f, gla_gate_bias_f, gla_gate_up_b, gla_gate_bias_b, gla_norm_g,
           att_norm_g, w_out, norm2_g, peer_w_q, peer_sub_keys, peer_u, peer_v, final_norm_g):
    B, S, D = x.shape
    assert norm1_g.shape[0] == 1, "single trunk layer: the final norm is fused with the layer's last residual"
    h, peer_t = _layer(x, norm1_g[0], w_in[0], gla_gate_up_f[0], gla_gate_bias_f[0], gla_gate_up_b[0],
                       gla_gate_bias_b[0], gla_norm_g[0], att_norm_g[0], w_out[0], norm2_g[0],
                       peer_w_q[0], peer_sub_keys[0], peer_u[0], peer_v[0])
    return _final(h, peer_t, final_norm_g.reshape(1, D)).reshape(B, S, D)
```

```python
import functools

import numpy as np
import jax
import jax.numpy as jnp
from jax import lax
from jax.experimental import pallas as pl
from jax.experimental.pallas import tpu as pltpu

F32 = jnp.float32
BF16 = jnp.bfloat16
I32 = jnp.int32

D_MODEL = 2048
GLA_HEADS = 4
GLA_DK = 128
GLA_DV = 256
GLA_RANK = 16
GLA_TAU = 16.0
GLA_CHUNK = 64
GLA_QK = GLA_HEADS * GLA_DK
GLA_V = GLA_HEADS * GLA_DV
ATT_HEADS = 16
ATT_HD = 64
ATT_W = ATT_HEADS * ATT_HD
ATT_DILATIONS = (1, 4, 16)
ATT_RADIUS = 64
PEER_NKEYS = 128
PEER_HEADS = 8
PEER_TOPK = 16
PEER_EXPERTS = PEER_NKEYS * PEER_NKEYS
PEER_HALF = 128
EPS = 1e-6
NEG = -1e30
MAIN_W = 2 * GLA_QK + 2 * GLA_V + 3 * ATT_W
GD_W = 2 * GLA_RANK

LANES = 128
SUBLANES = 8
VMEM_LIMIT = 56 * 1024 * 1024

NT = (((1,), (1,)), ((), ()))
TN = (((0,), (0,)), ((), ()))


def _cparams(n_axes, flags=None):
    return pltpu.CompilerParams(dimension_semantics=("arbitrary",) * n_axes,
                                vmem_limit_bytes=VMEM_LIMIT, flags=flags)


def _rms(x, g):
    ms = jnp.mean(x * x, axis=-1, keepdims=True)
    return x * lax.rsqrt(ms + EPS) * g


def _inproj_body(x_ref, g_ref, w_ref, wgd_ref, o_ref, gd_ref, xn_ref):
    @pl.when(pl.program_id(1) == 0)
    def _():
        xn = _rms(x_ref[...], g_ref[...]).astype(BF16)
        xn_ref[...] = xn
        gd_ref[...] = jnp.dot(xn, wgd_ref[...], preferred_element_type=F32)

    o_ref[...] = jnp.dot(xn_ref[...], w_ref[...], preferred_element_type=F32)


def _inproj(x2, g, w_main, w_gd, tm=1024, tn=1536):
    T = x2.shape[0]
    return pl.pallas_call(
        _inproj_body,
        out_shape=(jax.ShapeDtypeStruct((T, MAIN_W), F32), jax.ShapeDtypeStruct((T, GD_W), F32)),
        grid=(T // tm, MAIN_W // tn),
        in_specs=[pl.BlockSpec((tm, D_MODEL), lambda i, j: (i, 0)),
                  pl.BlockSpec((1, D_MODEL), lambda i, j: (0, 0)),
                  pl.BlockSpec((D_MODEL, tn), lambda i, j: (0, j)),
                  pl.BlockSpec((D_MODEL, GD_W), lambda i, j: (0, 0))],
        out_specs=(pl.BlockSpec((tm, tn), lambda i, j: (i, j)),
                   pl.BlockSpec((tm, GD_W), lambda i, j: (i, 0))),
        scratch_shapes=[pltpu.VMEM((tm, D_MODEL), BF16)],
        compiler_params=_cparams(2),
        name="inproj",
    )(x2, g, w_main, w_gd)


def _log_gate(gd, up_ref, bias_ref):
    z = jnp.dot(gd.astype(BF16), up_ref[...].astype(BF16), preferred_element_type=F32) + bias_ref[...]
    return (jnp.minimum(z, 0.0) - jnp.log1p(jnp.exp(-jnp.abs(z)))) * (1.0 / GLA_TAU)


def _gla_chunk(q_ref, k_ref, v_ref, la_ref, st_ref, o_ref, r0, tri_mask, tot_row):
    C = GLA_CHUNK
    sl = pl.ds(pl.multiple_of(r0, C), C)
    la = la_ref[sl, :]
    hi = la.astype(BF16)
    lo = (la - hi.astype(F32)).astype(BF16)
    tri = jnp.where(tri_mask, 1.0, 0.0).astype(BF16)
    b = (jnp.dot(tri, hi, preferred_element_type=F32) + jnp.dot(tri, lo, preferred_element_type=F32))
    btot = b[tot_row:tot_row + 1, :]
    q = q_ref[sl, :]
    k = k_ref[sl, :]
    v = v_ref[sl, :].astype(BF16)
    qd = (q * (jnp.exp(b) * (GLA_DK ** -0.5))).astype(BF16)
    kd = (k * jnp.exp(-b)).astype(BF16)
    kr = (k * jnp.exp(btot - b)).astype(BF16)
    att = lax.dot_general(qd, kd, NT, preferred_element_type=F32)
    att = jnp.where(tri_mask, att, 0.0).astype(BF16)
    st = st_ref[...]
    o = (jnp.dot(att, v, preferred_element_type=F32)
         + lax.dot_general(qd, st.astype(BF16), NT, preferred_element_type=F32))
    o_ref[sl, :] = o
    st_ref[...] = jnp.exp(btot) * st + lax.dot_general(v, kr, TN, preferred_element_type=F32)


def _gla_body(qf, kf, vf, gdf, qb, kb, vb, gdb, upf, bsf, upb, bsb, of_ref, ob_ref,
              stf, stb, laf, lab, *, nchunk):
    @pl.when(pl.program_id(2) == 0)
    def _():
        stf[...] = jnp.zeros_like(stf)
        stb[...] = jnp.zeros_like(stb)

    laf[...] = _log_gate(gdf[:, 0:GLA_RANK], upf, bsf)
    lab[...] = _log_gate(gdb[:, GLA_RANK:2 * GLA_RANK], upb, bsb)
    C = GLA_CHUNK
    row = lax.broadcasted_iota(I32, (C, C), 0)
    col = lax.broadcasted_iota(I32, (C, C), 1)
    lower = row >= col
    upper = col >= row

    def step(c, carry):
        _gla_chunk(qf, kf, vf, laf, stf, of_ref, c * C, lower, C - 1)
        _gla_chunk(qb, kb, vb, lab, stb, ob_ref, (nchunk - 1 - c) * C, upper, 0)
        return carry

    lax.fori_loop(0, nchunk, step, 0, unroll=16)


def _gla(main3, gd3, up_f, bias_f, up_b, bias_b, lb=1024):
    B, S, _ = main3.shape
    nb = S // lb
    kq = GLA_QK // GLA_DK
    kv = (2 * GLA_QK) // GLA_DV
    fwd = lambda b, h, i: (b, i, h)
    blk = lambda c0: (lambda b, h, i: (b, i, c0 + h))
    rblk = lambda c0: (lambda b, h, i: (b, nb - 1 - i, c0 + h))
    sq = lambda m: pl.BlockSpec((None, lb, GLA_DK), m)
    sv = lambda m: pl.BlockSpec((None, lb, GLA_DV), m)
    sg = lambda m: pl.BlockSpec((None, lb, GD_W), m)
    up_spec = pl.BlockSpec((GLA_RANK, GLA_DK), lambda b, h, i: (0, h))
    bias_spec = pl.BlockSpec((1, GLA_DK), lambda b, h, i: (0, h))
    out_sds = jax.ShapeDtypeStruct((B, S, GLA_V), F32)
    return pl.pallas_call(
        functools.partial(_gla_body, nchunk=lb // GLA_CHUNK),
        out_shape=(out_sds, out_sds),
        grid=(B, GLA_HEADS, nb),
        in_specs=[sq(blk(0)), sq(blk(kq)), sv(blk(kv)), sg(lambda b, h, i: (b, i, 0)),
                  sq(rblk(0)), sq(rblk(kq)), sv(rblk(kv)), sg(lambda b, h, i: (b, nb - 1 - i, 0)),
                  up_spec, bias_spec, up_spec, bias_spec],
        out_specs=(pl.BlockSpec((None, lb, GLA_DV), fwd),
                   pl.BlockSpec((None, lb, GLA_DV), lambda b, h, i: (b, nb - 1 - i, h))),
        scratch_shapes=[pltpu.VMEM((GLA_DV, GLA_DK), F32), pltpu.VMEM((GLA_DV, GLA_DK), F32),
                        pltpu.VMEM((lb, GLA_DK), F32), pltpu.VMEM((lb, GLA_DK), F32)],
        compiler_params=_cparams(3),
        name="gla",
    )(main3, main3, main3, gd3, main3, main3, main3, gd3, up_f, bias_f, up_b, bias_b)


ATT_TQ = 128
ATT_WIN = 256


ATT_GROUP = 32


ATT_FACTOR = 4
assert all(b == a * ATT_FACTOR for a, b in zip(ATT_DILATIONS[:-1], ATT_DILATIONS[1:])) and ATT_DILATIONS[0] == 1


def _split_classes(src, dst, d, n):
    f = ATT_FACTOR
    sub = n // (d * f)
    for r in range(d):
        for a in range(f):
            dst[pl.ds((d * a + r) * sub, sub), :] = src[pl.ds(r * (n // d) + a, sub, stride=f), :]


def _merge_classes(src, dst, d, n):
    f = ATT_FACTOR
    sub = n // (d * f)
    for r in range(d):
        for a in range(f):
            dst[pl.ds(r * (n // d) + a, sub, stride=f), :] = src[pl.ds((d * a + r) * sub, sub), :]


def _attn_body(slopes_ref, q_ref, k_ref, v_ref, o_ref, *scr, seq):
    pair = pl.program_id(1)
    lane = lax.broadcasted_iota(I32, (1, LANES), 1)
    head0 = lane < ATT_HD
    ii = lax.broadcasted_iota(I32, (ATT_TQ, ATT_WIN), 0)
    jj = lax.broadcasted_iota(I32, (ATT_TQ, ATT_WIN), 1)
    slope = (slopes_ref[2 * pair], slopes_ref[2 * pair + 1])
    npat = len(ATT_DILATIONS)
    copies = [(q_ref, k_ref, v_ref)] + [scr[3 * i:3 * i + 3] for i in range(npat - 1)]
    stats = [scr[3 * (npat - 1):3 * npat], scr[3 * npat:3 * (npat + 1)]]
    bias_s = scr[3 * (npat + 1)]
    for pi in range(1, npat):
        for x in range(3):
            _split_classes(copies[pi - 1][x], copies[pi][x], ATT_DILATIONS[pi - 1], seq)

    for step, pi in enumerate(reversed(range(npat))):
        d = ATT_DILATIONS[pi]
        L = seq // d
        nblk = L // ATT_TQ
        first = step == 0
        q_ref, k_ref, v_ref = copies[pi]
        if not first:
            for x in range(3):
                _merge_classes(stats[(step - 1) % 2][x], stats[step % 2][x], d, seq)
        m_s, l_s, acc_s = stats[step % 2]
        for case in range(3):
            ad = jnp.abs(jj - ii - case * ATT_RADIUS)
            for hh in range(2):
                bias_s[case, hh] = jnp.where(ad <= ATT_RADIUS, -slope[hh] * (ad * d).astype(F32), NEG)

        def group(gi, carry, L=L, nblk=nblk, first=first, q_ref=q_ref, k_ref=k_ref, v_ref=v_ref,
                  m_s=m_s, l_s=l_s, acc_s=acc_s):
            loaded = []
            for u in range(ATT_GROUP):
                t = gi * ATT_GROUP + u
                base = (t // nblk) * L
                q0 = (t % nblk) * ATT_TQ
                start = jnp.clip(q0 - ATT_RADIUS, 0, L - ATT_WIN)
                case = (q0 - start) // ATT_RADIUS
                qsl = pl.ds(pl.multiple_of(base + q0, ATT_TQ), ATT_TQ)
                ksl = pl.ds(pl.multiple_of(base + start, ATT_RADIUS), ATT_WIN)
                q = q_ref[qsl, :] * (ATT_HD ** -0.5)
                k = k_ref[ksl, :].astype(BF16)
                v = v_ref[ksl, :].astype(BF16)
                run = None if first else (m_s[qsl, :], l_s[qsl, :], acc_s[qsl, :])
                loaded.append((qsl, case, q, k, v, run))
            results = []
            for qsl, case, q, k, v, run in loaded:
                m_new, lsum, pv = [], [], []
                for hh in range(2):
                    hm = head0 if hh == 0 else jnp.logical_not(head0)
                    qh = jnp.where(hm, q, 0.0).astype(BF16)
                    s = lax.dot_general(qh, k, NT, preferred_element_type=F32) + bias_s[case, hh]
                    mh = jnp.max(s, axis=-1, keepdims=True)
                    p = jnp.exp(s - mh)
                    m_new.append(mh)
                    lsum.append(jnp.sum(p, axis=-1, keepdims=True))
                    pv.append(jnp.dot(p.astype(BF16), v, preferred_element_type=F32))
                m_full = jnp.where(head0, m_new[0], m_new[1])
                l_blk = jnp.where(head0, lsum[0], lsum[1])
                pv_blk = jnp.where(head0, pv[0], pv[1])
                if not first:
                    m_blk = m_full
                    m_full = jnp.maximum(run[0], m_blk)
                    a_run = jnp.exp(run[0] - m_full)
                    a_blk = jnp.exp(m_blk - m_full)
                    l_blk = a_run * run[1] + a_blk * l_blk
                    pv_blk = a_run * run[2] + a_blk * pv_blk
                results.append((qsl, m_full, l_blk, pv_blk))
            for qsl, m_full, l_blk, pv_blk in results:
                m_s[qsl, :] = m_full
                l_s[qsl, :] = l_blk
                acc_s[qsl, :] = pv_blk
            return carry

        lax.fori_loop(0, d * nblk // ATT_GROUP, group, 0)

    o_ref[...] = acc_s[...] / l_s[...]


def _attn(main3, slopes):
    B, S, _ = main3.shape
    c0 = (2 * GLA_QK + 2 * GLA_V) // LANES
    npair = ATT_HEADS // 2
    spec = lambda off: pl.BlockSpec((None, S, LANES), lambda b, p: (b, 0, c0 + off + p))
    return pl.pallas_call(
        functools.partial(_attn_body, seq=S),
        out_shape=jax.ShapeDtypeStruct((B, S, ATT_W), F32),
        grid=(B, npair),
        in_specs=[pl.BlockSpec(memory_space=pltpu.SMEM), spec(0), spec(npair), spec(2 * npair)],
        out_specs=pl.BlockSpec((None, S, LANES), lambda b, p: (b, 0, p)),
        scratch_shapes=([pltpu.VMEM((S, LANES), F32)] * (3 * (len(ATT_DILATIONS) + 1))
                        + [pltpu.VMEM((3, 2, ATT_TQ, ATT_WIN), F32)]),
        compiler_params=_cparams(2),
        name="attn",
    )(slopes, main3, main3, main3)


def _mix_body(of_ref, ob_ref, gr_ref, att_ref, x_ref, gg_ref, ag_ref, wo_ref, n2_ref, h_ref, hn_ref):
    o = of_ref[...] + ob_ref[...]
    parts = [_rms(o[:, h * GLA_DV:(h + 1) * GLA_DV], gg_ref[...]) for h in range(GLA_HEADS)]
    on = jnp.concatenate(parts, axis=-1)
    gr = gr_ref[...]
    gla = on * (gr / (1.0 + jnp.exp(-gr)))
    an = _rms(att_ref[...], ag_ref[...])
    y = (jnp.dot(gla.astype(BF16), wo_ref[0:GLA_V, :], preferred_element_type=F32)
         + jnp.dot(an.astype(BF16), wo_ref[GLA_V:GLA_V + ATT_W, :], preferred_element_type=F32))
    h = x_ref[...] + y
    h_ref[...] = h
    hn_ref[...] = _rms(h, n2_ref[...]).astype(BF16)


def _mix(o_f, o_b, main2, att2, x2, gg, ag, w_out, n2g, tm=512):
    T = x2.shape[0]
    gr_blk = (2 * GLA_QK + GLA_V) // GLA_V
    row = lambda i: (i, 0)
    const = lambda i: (0, 0)
    return pl.pallas_call(
        _mix_body,
        out_shape=(jax.ShapeDtypeStruct((T, D_MODEL), F32), jax.ShapeDtypeStruct((T, D_MODEL), BF16)),
        grid=(T // tm,),
        in_specs=[pl.BlockSpec((tm, GLA_V), row), pl.BlockSpec((tm, GLA_V), row),
                  pl.BlockSpec((tm, GLA_V), lambda i: (i, gr_blk)),
                  pl.BlockSpec((tm, ATT_W), row), pl.BlockSpec((tm, D_MODEL), row),
                  pl.BlockSpec((1, GLA_DV), const), pl.BlockSpec((1, ATT_W), const),
                  pl.BlockSpec((D_MODEL, D_MODEL), const), pl.BlockSpec((1, D_MODEL), const)],
        out_specs=(pl.BlockSpec((tm, D_MODEL), row), pl.BlockSpec((tm, D_MODEL), row)),
        compiler_params=_cparams(1),
        name="mix",
    )(o_f, o_b, main2, att2, x2, gg, ag, w_out, n2g)


ROUTE_TM = SUBLANES * LANES
_CELLS = [(a, b) for a in range(PEER_TOPK) for b in range(PEER_TOPK) if (a + 1) * (b + 1) <= PEER_TOPK]


def _key_rows(i):
    return pl.ds(i * SUBLANES, SUBLANES)


def _before(a, b):
    (va, ia), (vb, ib) = a, b
    if isinstance(ia, float) and isinstance(ib, float):
        return (va >= vb) if ia < ib else (va > vb)
    return jnp.logical_or(va > vb, jnp.logical_and(va == vb, ia < ib))


def _pick(c, a, b):
    return tuple(jnp.where(c, x, y) for x, y in zip(a, b))


def _ordered_pair(a, b):
    c = _before(a, b)
    return _pick(c, a, b), _pick(c, b, a)


def _bitonic_merge(xs):
    n = len(xs)
    if n == 1:
        return list(xs)
    h = n // 2
    xs = list(xs)
    for k in range(h):
        xs[k], xs[k + h] = _ordered_pair(xs[k], xs[k + h])
    return _bitonic_merge(xs[:h]) + _bitonic_merge(xs[h:])


def _bitonic_sort(xs):
    n = len(xs)
    if n == 1:
        return list(xs)
    return _bitonic_merge(_bitonic_sort(xs[:n // 2]) + _bitonic_sort(xs[n // 2:])[::-1])


def _top_keys(s_ref):
    k = PEER_TOPK
    best = None
    for g in range(PEER_NKEYS // k):
        grp = _bitonic_sort([(s_ref[_key_rows(g * k + a), :], float(g * k + a)) for a in range(k)])
        grp = [(v, jnp.full((SUBLANES, LANES), i, F32) if isinstance(i, float) else i) for v, i in grp]
        if best is None:
            best = grp
        else:
            best = _bitonic_merge([_pick(_before(best[j], grp[k - 1 - j]), best[j], grp[k - 1 - j])
                                   for j in range(k)])
    return [v for v, _ in best], [i for _, i in best]


def _to_token_lanes(src_ref, dst_ref):
    for g in range(SUBLANES):
        dst_ref[g] = src_ref[pl.ds(g, PEER_NKEYS, stride=SUBLANES), :].astype(dst_ref.dtype)


def _route_body(hn_ref, wq_ref, keys_ref, r2_ref, e2_ref, n1_ref, c1_ref, s1_s, s2_s, rk2_s, n1_s):
    q = jnp.dot(hn_ref[...], wq_ref[...], preferred_element_type=F32)
    for c, s_s in enumerate((s1_s, s2_s)):
        qc = q[:, c * PEER_HALF:(c + 1) * PEER_HALF].astype(BF16)
        st = lax.dot_general(keys_ref[c], qc, NT, preferred_element_type=F32)
        for g in range(SUBLANES):
            s_s[pl.ds(g, PEER_NKEYS, stride=SUBLANES), :] = st[:, g * LANES:(g + 1) * LANES]
    sc1, si1 = _top_keys(s1_s)
    sc2, si2 = _top_keys(s2_s)
    e1 = [jnp.exp(sc1[k] - sc1[0]) for k in range(PEER_TOPK)]
    e2 = [jnp.exp(sc2[k] - sc2[0]) for k in range(PEER_TOPK)]

    cand = [sc1[a] + sc2[b] for (a, b) in _CELLS]
    nc = len(_CELLS)
    before = [jnp.zeros((SUBLANES, LANES), F32) for _ in range(nc)]
    for x in range(nc):
        for y in range(x + 1, nc):
            ax, bx = _CELLS[x]
            ay, by = _CELLS[y]
            if ax <= ay and bx <= by:
                before[y] = before[y] + 1.0
                continue
            bt = jnp.where(cand[x] >= cand[y], 1.0, 0.0)
            before[y] = before[y] + bt
            before[x] = before[x] + (1.0 - bt)
    zero = jnp.zeros((SUBLANES, LANES), F32)
    ncol = [zero] * PEER_TOPK
    zsum = zero
    for x, (a, b) in enumerate(_CELLS):
        sel = jnp.where(before[x] < float(PEER_TOPK), 1.0, 0.0)
        ncol[a] = ncol[a] + sel
        zsum = zsum + sel * (e1[a] * e2[b])
    inv_z = 1.0 / zsum

    def dense(i, carry):
        rows = _key_rows(i)
        key = lax.convert_element_type(i, F32)
        n1 = zero
        rk2 = jnp.full((SUBLANES, LANES), float(PEER_TOPK), F32)
        for k in range(PEER_TOPK):
            n1 = jnp.where(si1[k] == key, ncol[k], n1)
            rk2 = jnp.where(si2[k] == key, float(k), rk2)
        n1_s[rows, :] = n1
        rk2_s[rows, :] = rk2
        s1_s[rows, :] = jnp.exp(s1_s[rows, :] - sc1[0]) * inv_z
        s2_s[rows, :] = jnp.exp(s2_s[rows, :] - sc2[0])
        return carry

    lax.fori_loop(0, PEER_NKEYS, dense, 0, unroll=4)
    _to_token_lanes(rk2_s, r2_ref)
    _to_token_lanes(s2_s, e2_ref)
    _to_token_lanes(n1_s, n1_ref)
    _to_token_lanes(s1_s, c1_ref)


def _route(hn, w_q, keys):
    T = hn.shape[0]
    tm = ROUTE_TM
    out_sds = lambda dt: jax.ShapeDtypeStruct((PEER_HEADS, T // LANES, PEER_NKEYS, LANES), dt)
    out_spec = pl.BlockSpec((None, SUBLANES, PEER_NKEYS, LANES), lambda i, h: (h, i, 0, 0))
    km = pltpu.VMEM((PEER_NKEYS * SUBLANES, LANES), F32)
    return pl.pallas_call(
        _route_body,
        out_shape=(out_sds(F32),) * 4,
        grid=(T // tm, PEER_HEADS),
        in_specs=[pl.BlockSpec((tm, D_MODEL), lambda i, h: (i, 0)),
                  pl.BlockSpec((D_MODEL, 2 * PEER_HALF), lambda i, h: (0, h)),
                  pl.BlockSpec((None, 2, PEER_NKEYS, PEER_HALF), lambda i, h: (h, 0, 0, 0))],
        out_specs=(out_spec,) * 4,
        scratch_shapes=[km] * 4,
        compiler_params=_cparams(2),
        name="route",
    )(hn, w_q, keys)


PEER_TE = 512
PEER_QROWS = 32
INV_SQRT2 = 0.7071067811865476


def _peer_tile(a_prev, a_next, jprev, live, hn_ref, u_ref, vt_ref, r2_ref, e2_ref, n1_ref, c1_ref, o_ref, g_scr):
    n_il = PEER_TE // PEER_NKEYS
    packed = 2 * SUBLANES
    reps = PEER_QROWS // packed
    n_qd = PEER_NKEYS // PEER_QROWS
    half = 0.5 * live
    a_next[...] = lax.dot_general(u_ref[...], hn_ref[...], NT, preferred_element_type=F32)
    for il0 in range(0, n_il, 2):
        ils = (il0, il0 + 1)
        for cb in range(g_scr.shape[1] // LANES):
            lanes = slice(cb * LANES, (cb + 1) * LANES)
            w = {(il, qd): None for il in ils for qd in range(n_qd)}
            for h in range(PEER_HEADS):
                row = {}
                for il in ils:
                    key = pl.ds(jprev * n_il + il, packed, stride=0)
                    row[il] = (jnp.concatenate([n1_ref[h, cb, key, :].astype(BF16)] * reps, axis=0),
                               jnp.concatenate([c1_ref[h, cb, key, :].astype(BF16)] * reps, axis=0))
                for qd in range(n_qd):
                    rows = slice(qd * PEER_QROWS, (qd + 1) * PEER_QROWS)
                    r2 = r2_ref[h, cb, rows, :]
                    e2 = e2_ref[h, cb, rows, :]
                    for il in ils:
                        t = jnp.where(r2 < row[il][0], e2, jnp.zeros_like(e2)) * row[il][1]
                        w[il, qd] = t if w[il, qd] is None else w[il, qd] + t
            for il in ils:
                for qd in range(n_qd):
                    er = slice(il * PEER_NKEYS + qd * PEER_QROWS, il * PEER_NKEYS + (qd + 1) * PEER_QROWS)
                    a = a_prev[er, lanes]
                    act = half * a * (1.0 + lax.erf(a * INV_SQRT2))
                    g_scr[er, lanes] = act.astype(BF16) * w[il, qd]
    o_ref[...] += jnp.dot(vt_ref[...], g_scr[...], preferred_element_type=F32)


def _peer_body(hn_ref, u_ref, vt_ref, r2_ref, e2_ref, n1_ref, c1_ref, o_ref, a0, a1, g_scr, r2_b, e2_b):
    i = pl.program_id(0)
    j = pl.program_id(1)

    @pl.when(jnp.logical_and(i == 0, j == 0))
    def _():
        a1[...] = jnp.zeros_like(a1)

    @pl.when(j == 0)
    def _():
        o_ref[...] = jnp.zeros_like(o_ref)
        r2_b[...] = r2_ref[...].astype(BF16)
        e2_b[...] = e2_ref[...].astype(BF16)

    live = jnp.where(j > 0, 1.0, 0.0).astype(F32)
    jprev = jnp.maximum(j - 1, 0)
    tile = functools.partial(_peer_tile, jprev=jprev, live=live, hn_ref=hn_ref, u_ref=u_ref, vt_ref=vt_ref,
                             r2_ref=r2_b, e2_ref=e2_b, n1_ref=n1_ref, c1_ref=c1_ref, o_ref=o_ref, g_scr=g_scr)

    @pl.when(j % 2 == 0)
    def _():
        tile(a1, a0)

    @pl.when(j % 2 == 1)
    def _():
        tile(a0, a1)


def _peer(hn, u, vt, r2, e2, n1, c1, tm=512):
    T = hn.shape[0]
    te = PEER_TE
    nj = PEER_EXPERTS // te
    rspec = pl.BlockSpec((PEER_HEADS, tm // LANES, PEER_NKEYS, LANES), lambda i, j: (0, i, 0, 0))
    return pl.pallas_call(
        _peer_body,
        out_shape=jax.ShapeDtypeStruct((D_MODEL, T), F32),
        grid=(T // tm, nj + 1),
        in_specs=[pl.BlockSpec((tm, D_MODEL), lambda i, j: (i, 0)),
                  pl.BlockSpec((te, D_MODEL), lambda i, j: (jnp.minimum(j, nj - 1), 0)),
                  pl.BlockSpec((None, D_MODEL, te), lambda i, j: (jnp.maximum(j - 1, 0), 0, 0)),
                  rspec, rspec, rspec, rspec],
        out_specs=pl.BlockSpec((D_MODEL, tm), lambda i, j: (0, i)),
        scratch_shapes=([pltpu.VMEM((te, tm), F32), pltpu.VMEM((te, tm), F32), pltpu.VMEM((te, tm), BF16)]
                        + [pltpu.VMEM((PEER_HEADS, tm // LANES, PEER_NKEYS, LANES), BF16)] * 2),
        compiler_params=_cparams(2),
        name="peer",
    )(hn, u, vt, r2, e2, n1, c1)


def _final_body(h_ref, pt_ref, g_ref, o_ref):
    o_ref[...] = _rms(h_ref[...] + pt_ref[...].T, g_ref[...])


def _final(h, peer_t, g, tm=512):
    T = h.shape[0]
    return pl.pallas_call(
        _final_body,
        out_shape=jax.ShapeDtypeStruct((T, D_MODEL), F32),
        grid=(T // tm,),
        in_specs=[pl.BlockSpec((tm, D_MODEL), lambda i: (i, 0)),
                  pl.BlockSpec((D_MODEL, tm), lambda i: (0, i)),
                  pl.BlockSpec((1, D_MODEL), lambda i: (0, 0))],
        out_specs=pl.BlockSpec((tm, D_MODEL), lambda i: (i, 0)),
        compiler_params=_cparams(1),
        name="final",
    )(h, peer_t, g)


def _layer(x, norm1_g, w_in, up_f, bias_f, up_b, bias_b, gla_norm_g, att_norm_g, w_out, norm2_g,
           w_q, sub_keys, peer_u, peer_v):
    B, S, D = x.shape
    T = B * S
    x2 = x.reshape(T, D)
    gd0 = 2 * GLA_QK + 2 * GLA_V
    w_main = jnp.concatenate([w_in[:, :gd0], w_in[:, gd0 + GD_W:]], axis=1).astype(BF16)
    w_gd = w_in[:, gd0:gd0 + GD_W].astype(BF16)
    main, gd = _inproj(x2, norm1_g.reshape(1, D), w_main, w_gd)
    main3 = main.reshape(B, S, MAIN_W)
    o_f, o_b = _gla(main3, gd.reshape(B, S, GD_W), up_f, bias_f.reshape(1, GLA_QK),
                    up_b, bias_b.reshape(1, GLA_QK))
    slopes = jnp.asarray((2.0 ** (-8.0 * np.arange(1, ATT_HEADS + 1) / ATT_HEADS)).astype(np.float32))
    att = _attn(main3, slopes)
    h, hn = _mix(o_f.reshape(T, GLA_V), o_b.reshape(T, GLA_V), main, att.reshape(T, ATT_W), x2,
                 gla_norm_g.reshape(1, GLA_DV), att_norm_g.reshape(1, ATT_W), w_out.astype(BF16),
                 norm2_g.reshape(1, D))
    r2, e2, n1, c1 = _route(hn, w_q.astype(BF16), sub_keys.astype(BF16))
    vt = peer_v.astype(BF16).reshape(PEER_EXPERTS // PEER_TE, PEER_TE, D).transpose(0, 2, 1)
    peer_t = _peer(hn, peer_u.astype(BF16), vt, r2, e2, n1, c1)
    return h, peer_t


def kernel(x, norm1_g, w_in, gla_gate_up_f, gla_gate_bias_f, gla_gate_up_b, gla_gate_bias_b, gla_norm_g,
           att_norm_g, w_out, norm2_g, peer_w_q, peer_sub_keys, peer_u, peer_v, final_norm_g):
    B, S, D = x.shape
    assert norm1_g.shape[0] == 1, "single trunk layer: the final norm is fused with the layer's last residual"
    h, peer_t = _layer(x, norm1_g[0], w_in[0], gla_gate_up_f[0], gla_gate_bias_f[0], gla_gate_up_b[0],
                       gla_gate_bias_b[0], gla_norm_g[0], att_norm_g[0], w_out[0], norm2_g[0],
                       peer_w_q[0], peer_sub_keys[0], peer_u[0], peer_v[0])
    return _final(h, peer_t, final_norm_g.reshape(1, D)).reshape(B, S, D)
```

```python
import functools

import numpy as np
import jax
import jax.numpy as jnp
from jax import lax
from jax.experimental import pallas as pl
from jax.experimental.pallas import tpu as pltpu

F32 = jnp.float32
BF16 = jnp.bfloat16
I32 = jnp.int32

D_MODEL = 2048
GLA_HEADS = 4
GLA_DK = 128
GLA_DV = 256
GLA_RANK = 16
GLA_TAU = 16.0
GLA_CHUNK = 64
GLA_QK = GLA_HEADS * GLA_DK
GLA_V = GLA_HEADS * GLA_DV
ATT_HEADS = 16
ATT_HD = 64
ATT_W = ATT_HEADS * ATT_HD
ATT_DILATIONS = (1, 4, 16)
ATT_RADIUS = 64
PEER_NKEYS = 128
PEER_HEADS = 8
PEER_TOPK = 16
PEER_EXPERTS = PEER_NKEYS * PEER_NKEYS
PEER_HALF = 128
EPS = 1e-6
NEG = -1e30
MAIN_W = 2 * GLA_QK + 2 * GLA_V + 3 * ATT_W
GD_W = 2 * GLA_RANK

LANES = 128
SUBLANES = 8
VMEM_LIMIT = 56 * 1024 * 1024

NT = (((1,), (1,)), ((), ()))
TN = (((0,), (0,)), ((), ()))


def _cparams(n_axes, flags=None):
    return pltpu.CompilerParams(dimension_semantics=("arbitrary",) * n_axes,
                                vmem_limit_bytes=VMEM_LIMIT, flags=flags)


def _rms(x, g):
    ms = jnp.mean(x * x, axis=-1, keepdims=True)
    return x * lax.rsqrt(ms + EPS) * g


def _inproj_body(x_ref, g_ref, w_ref, wgd_ref, o_ref, gd_ref, xn_ref):
    @pl.when(pl.program_id(1) == 0)
    def _():
        xn = _rms(x_ref[...], g_ref[...]).astype(BF16)
        xn_ref[...] = xn
        gd_ref[...] = jnp.dot(xn, wgd_ref[...], preferred_element_type=F32)

    o_ref[...] = jnp.dot(xn_ref[...], w_ref[...], preferred_element_type=F32)


def _inproj(x2, g, w_main, w_gd, tm=1024, tn=1536):
    T = x2.shape[0]
    return pl.pallas_call(
        _inproj_body,
        out_shape=(jax.ShapeDtypeStruct((T, MAIN_W), F32), jax.ShapeDtypeStruct((T, GD_W), F32)),
        grid=(T // tm, MAIN_W // tn),
        in_specs=[pl.BlockSpec((tm, D_MODEL), lambda i, j: (i, 0)),
                  pl.BlockSpec((1, D_MODEL), lambda i, j: (0, 0)),
                  pl.BlockSpec((D_MODEL, tn), lambda i, j: (0, j)),
                  pl.BlockSpec((D_MODEL, GD_W), lambda i, j: (0, 0))],
        out_specs=(pl.BlockSpec((tm, tn), lambda i, j: (i, j)),
                   pl.BlockSpec((tm, GD_W), lambda i, j: (i, 0))),
        scratch_shapes=[pltpu.VMEM((tm, D_MODEL), BF16)],
        compiler_params=_cparams(2),
        name="inproj",
    )(x2, g, w_main, w_gd)


def _log_gate(gd, up_ref, bias_ref):
    z = jnp.dot(gd.astype(BF16), up_ref[...].astype(BF16), preferred_element_type=F32) + bias_ref[...]
    return (jnp.minimum(z, 0.0) - jnp.log1p(jnp.exp(-jnp.abs(z)))) * (1.0 / GLA_TAU)


def _chunk_cumsum(la, tri_mask, b_ref, nchunk):
    C = GLA_CHUNK
    hi = la.astype(BF16)
    lo = (la - hi.astype(F32)).astype(BF16)
    cols = []
    for c in range(nchunk):
        cols += [hi[c * C:(c + 1) * C, :], lo[c * C:(c + 1) * C, :]]
    tri = jnp.where(tri_mask, 1.0, 0.0).astype(BF16)
    both = jnp.dot(tri, jnp.concatenate(cols, axis=1), preferred_element_type=F32)
    for c in range(nchunk):
        b_ref[c * C:(c + 1) * C, :] = (both[:, 2 * c * GLA_DK:(2 * c + 1) * GLA_DK]
                                       + both[:, (2 * c + 1) * GLA_DK:(2 * c + 2) * GLA_DK])


def _gla_chunk(q_ref, k_ref, v_ref, b_ref, st_ref, o_ref, r0, tri_mask, tot_row):
    C = GLA_CHUNK
    sl = pl.ds(pl.multiple_of(r0, C), C)
    b = b_ref[sl, :]
    btot = b[tot_row:tot_row + 1, :]
    q = q_ref[sl, :]
    k = k_ref[sl, :]
    v = v_ref[sl, :].astype(BF16)
    qd = (q * (jnp.exp(b) * (GLA_DK ** -0.5))).astype(BF16)
    kd = (k * jnp.exp(-b)).astype(BF16)
    kr = (k * jnp.exp(btot - b)).astype(BF16)
    att = lax.dot_general(qd, kd, NT, preferred_element_type=F32)
    att = jnp.where(tri_mask, att, 0.0).astype(BF16)
    st = st_ref[...]
    o = (jnp.dot(att, v, preferred_element_type=F32)
         + lax.dot_general(qd, st.astype(BF16), NT, preferred_element_type=F32))
    o_ref[sl, :] = o
    st_ref[...] = jnp.exp(btot) * st + lax.dot_general(v, kr, TN, preferred_element_type=F32)


def _gla_body(qf, kf, vf, gdf, qb, kb, vb, gdb, upf, bsf, upb, bsb, of_ref, ob_ref,
              stf, stb, laf, lab, *, nchunk):
    @pl.when(pl.program_id(2) == 0)
    def _():
        stf[...] = jnp.zeros_like(stf)
        stb[...] = jnp.zeros_like(stb)

    C = GLA_CHUNK
    row = lax.broadcasted_iota(I32, (C, C), 0)
    col = lax.broadcasted_iota(I32, (C, C), 1)
    lower = row >= col
    upper = col >= row
    _chunk_cumsum(_log_gate(gdf[:, 0:GLA_RANK], upf, bsf), lower, laf, nchunk)
    _chunk_cumsum(_log_gate(gdb[:, GLA_RANK:2 * GLA_RANK], upb, bsb), upper, lab, nchunk)

    def step(c, carry):
        _gla_chunk(qf, kf, vf, laf, stf, of_ref, c * C, lower, C - 1)
        _gla_chunk(qb, kb, vb, lab, stb, ob_ref, (nchunk - 1 - c) * C, upper, 0)
        return carry

    lax.fori_loop(0, nchunk, step, 0, unroll=16)


def _gla(main3, gd3, up_f, bias_f, up_b, bias_b, lb=1024):
    B, S, _ = main3.shape
    nb = S // lb
    kq = GLA_QK // GLA_DK
    kv = (2 * GLA_QK) // GLA_DV
    fwd = lambda b, h, i: (b, i, h)
    blk = lambda c0: (lambda b, h, i: (b, i, c0 + h))
    rblk = lambda c0: (lambda b, h, i: (b, nb - 1 - i, c0 + h))
    sq = lambda m: pl.BlockSpec((None, lb, GLA_DK), m)
    sv = lambda m: pl.BlockSpec((None, lb, GLA_DV), m)
    sg = lambda m: pl.BlockSpec((None, lb, GD_W), m)
    up_spec = pl.BlockSpec((GLA_RANK, GLA_DK), lambda b, h, i: (0, h))
    bias_spec = pl.BlockSpec((1, GLA_DK), lambda b, h, i: (0, h))
    out_sds = jax.ShapeDtypeStruct((B, S, GLA_V), F32)
    return pl.pallas_call(
        functools.partial(_gla_body, nchunk=lb // GLA_CHUNK),
        out_shape=(out_sds, out_sds),
        grid=(B, GLA_HEADS, nb),
        in_specs=[sq(blk(0)), sq(blk(kq)), sv(blk(kv)), sg(lambda b, h, i: (b, i, 0)),
                  sq(rblk(0)), sq(rblk(kq)), sv(rblk(kv)), sg(lambda b, h, i: (b, nb - 1 - i, 0)),
                  up_spec, bias_spec, up_spec, bias_spec],
        out_specs=(pl.BlockSpec((None, lb, GLA_DV), fwd),
                   pl.BlockSpec((None, lb, GLA_DV), lambda b, h, i: (b, nb - 1 - i, h))),
        scratch_shapes=[pltpu.VMEM((GLA_DV, GLA_DK), F32), pltpu.VMEM((GLA_DV, GLA_DK), F32),
                        pltpu.VMEM((lb, GLA_DK), F32), pltpu.VMEM((lb, GLA_DK), F32)],
        compiler_params=_cparams(3),
        name="gla",
    )(main3, main3, main3, gd3, main3, main3, main3, gd3, up_f, bias_f, up_b, bias_b)


ATT_TQ = 128
ATT_WIN = 256


ATT_GROUP = 32


ATT_FACTOR = 4
assert all(b == a * ATT_FACTOR for a, b in zip(ATT_DILATIONS[:-1], ATT_DILATIONS[1:])) and ATT_DILATIONS[0] == 1


def _split_classes(src, dst, d, n):
    f = ATT_FACTOR
    sub = n // (d * f)
    for r in range(d):
        for a in range(f):
            dst[pl.ds((d * a + r) * sub, sub), :] = src[pl.ds(r * (n // d) + a, sub, stride=f), :]


def _merge_classes(src, dst, d, n):
    f = ATT_FACTOR
    sub = n // (d * f)
    for r in range(d):
        for a in range(f):
            dst[pl.ds(r * (n // d) + a, sub, stride=f), :] = src[pl.ds((d * a + r) * sub, sub), :]


def _attn_body(slopes_ref, q_ref, k_ref, v_ref, o_ref, *scr, seq):
    pair = pl.program_id(1)
    lane = lax.broadcasted_iota(I32, (1, LANES), 1)
    head0 = lane < ATT_HD
    ii = lax.broadcasted_iota(I32, (ATT_TQ, ATT_WIN), 0)
    jj = lax.broadcasted_iota(I32, (ATT_TQ, ATT_WIN), 1)
    slope = (slopes_ref[2 * pair], slopes_ref[2 * pair + 1])
    npat = len(ATT_DILATIONS)
    copies = [(q_ref, k_ref, v_ref)] + [scr[3 * i:3 * i + 3] for i in range(npat - 1)]
    stats = [scr[3 * (npat - 1):3 * npat], scr[3 * npat:3 * (npat + 1)]]
    bias_s = scr[3 * (npat + 1)]
    for pi in range(1, npat):
        for x in range(3):
            _split_classes(copies[pi - 1][x], copies[pi][x], ATT_DILATIONS[pi - 1], seq)

    for step, pi in enumerate(reversed(range(npat))):
        d = ATT_DILATIONS[pi]
        L = seq // d
        nblk = L // ATT_TQ
        first = step == 0
        q_ref, k_ref, v_ref = copies[pi]
        if not first:
            for x in range(3):
                _merge_classes(stats[(step - 1) % 2][x], stats[step % 2][x], d, seq)
        m_s, l_s, acc_s = stats[step % 2]
        for case in range(3):
            ad = jnp.abs(jj - ii - case * ATT_RADIUS)
            for hh in range(2):
                bias_s[case, hh] = jnp.where(ad <= ATT_RADIUS, -slope[hh] * (ad * d).astype(F32), NEG)

        def group(gi, carry, L=L, nblk=nblk, first=first, q_ref=q_ref, k_ref=k_ref, v_ref=v_ref,
                  m_s=m_s, l_s=l_s, acc_s=acc_s):
            loaded = []
            for u in range(ATT_GROUP):
                t = gi * ATT_GROUP + u
                base = (t // nblk) * L
                q0 = (t % nblk) * ATT_TQ
                start = jnp.clip(q0 - ATT_RADIUS, 0, L - ATT_WIN)
                case = (q0 - start) // ATT_RADIUS
                qsl = pl.ds(pl.multiple_of(base + q0, ATT_TQ), ATT_TQ)
                ksl = pl.ds(pl.multiple_of(base + start, ATT_RADIUS), ATT_WIN)
                q = q_ref[qsl, :] * (ATT_HD ** -0.5)
                k = k_ref[ksl, :].astype(BF16)
                v = v_ref[ksl, :].astype(BF16)
                run = None if first else (m_s[qsl, :], l_s[qsl, :], acc_s[qsl, :])
                loaded.append((qsl, case, q, k, v, run))
            results = []
            for qsl, case, q, k, v, run in loaded:
                m_new, lsum, pv = [], [], []
                for hh in range(2):
                    hm = head0 if hh == 0 else jnp.logical_not(head0)
                    qh = jnp.where(hm, q, 0.0).astype(BF16)
                    s = lax.dot_general(qh, k, NT, preferred_element_type=F32) + bias_s[case, hh]
                    mh = jnp.max(s, axis=-1, keepdims=True)
                    p = jnp.exp(s - mh)
                    m_new.append(mh)
                    lsum.append(jnp.sum(p, axis=-1, keepdims=True))
                    pv.append(jnp.dot(p.astype(BF16), v, preferred_element_type=F32))
                m_full = jnp.where(head0, m_new[0], m_new[1])
                l_blk = jnp.where(head0, lsum[0], lsum[1])
                pv_blk = jnp.where(head0, pv[0], pv[1])
                if not first:
                    m_blk = m_full
                    m_full = jnp.maximum(run[0], m_blk)
                    a_run = jnp.exp(run[0] - m_full)
                    a_blk = jnp.exp(m_blk - m_full)
                    l_blk = a_run * run[1] + a_blk * l_blk
                    pv_blk = a_run * run[2] + a_blk * pv_blk
                results.append((qsl, m_full, l_blk, pv_blk))
            for qsl, m_full, l_blk, pv_blk in results:
                m_s[qsl, :] = m_full
                l_s[qsl, :] = l_blk
                acc_s[qsl, :] = pv_blk
            return carry

        lax.fori_loop(0, d * nblk // ATT_GROUP, group, 0)

    o_ref[...] = acc_s[...] / l_s[...]


def _attn(main3, slopes):
    B, S, _ = main3.shape
    c0 = (2 * GLA_QK + 2 * GLA_V) // LANES
    npair = ATT_HEADS // 2
    spec = lambda off: pl.BlockSpec((None, S, LANES), lambda b, p: (b, 0, c0 + off + p))
    return pl.pallas_call(
        functools.partial(_attn_body, seq=S),
        out_shape=jax.ShapeDtypeStruct((B, S, ATT_W), F32),
        grid=(B, npair),
        in_specs=[pl.BlockSpec(memory_space=pltpu.SMEM), spec(0), spec(npair), spec(2 * npair)],
        out_specs=pl.BlockSpec((None, S, LANES), lambda b, p: (b, 0, p)),
        scratch_shapes=([pltpu.VMEM((S, LANES), F32)] * (3 * (len(ATT_DILATIONS) + 1))
                        + [pltpu.VMEM((3, 2, ATT_TQ, ATT_WIN), F32)]),
        compiler_params=_cparams(2),
        name="attn",
    )(slopes, main3, main3, main3)


def _mix_body(of_ref, ob_ref, gr_ref, att_ref, x_ref, gg_ref, ag_ref, wo_ref, n2_ref, h_ref, hn_ref):
    o = of_ref[...] + ob_ref[...]
    parts = [_rms(o[:, h * GLA_DV:(h + 1) * GLA_DV], gg_ref[...]) for h in range(GLA_HEADS)]
    on = jnp.concatenate(parts, axis=-1)
    gr = gr_ref[...]
    gla = on * (gr / (1.0 + jnp.exp(-gr)))
    an = _rms(att_ref[...], ag_ref[...])
    y = (jnp.dot(gla.astype(BF16), wo_ref[0:GLA_V, :], preferred_element_type=F32)
         + jnp.dot(an.astype(BF16), wo_ref[GLA_V:GLA_V + ATT_W, :], preferred_element_type=F32))
    h = x_ref[...] + y
    h_ref[...] = h
    hn_ref[...] = _rms(h, n2_ref[...]).astype(BF16)


def _mix(o_f, o_b, main2, att2, x2, gg, ag, w_out, n2g, tm=512):
    T = x2.shape[0]
    gr_blk = (2 * GLA_QK + GLA_V) // GLA_V
    row = lambda i: (i, 0)
    const = lambda i: (0, 0)
    return pl.pallas_call(
        _mix_body,
        out_shape=(jax.ShapeDtypeStruct((T, D_MODEL), F32), jax.ShapeDtypeStruct((T, D_MODEL), BF16)),
        grid=(T // tm,),
        in_specs=[pl.BlockSpec((tm, GLA_V), row), pl.BlockSpec((tm, GLA_V), row),
                  pl.BlockSpec((tm, GLA_V), lambda i: (i, gr_blk)),
                  pl.BlockSpec((tm, ATT_W), row), pl.BlockSpec((tm, D_MODEL), row),
                  pl.BlockSpec((1, GLA_DV), const), pl.BlockSpec((1, ATT_W), const),
                  pl.BlockSpec((D_MODEL, D_MODEL), const), pl.BlockSpec((1, D_MODEL), const)],
        out_specs=(pl.BlockSpec((tm, D_MODEL), row), pl.BlockSpec((tm, D_MODEL), row)),
        compiler_params=_cparams(1),
        name="mix",
    )(o_f, o_b, main2, att2, x2, gg, ag, w_out, n2g)


ROUTE_TM = SUBLANES * LANES
_CELLS = [(a, b) for a in range(PEER_TOPK) for b in range(PEER_TOPK) if (a + 1) * (b + 1) <= PEER_TOPK]


def _key_rows(i):
    return pl.ds(i * SUBLANES, SUBLANES)


def _before(a, b):
    (va, ia), (vb, ib) = a, b
    if isinstance(ia, float) and isinstance(ib, float):
        return (va >= vb) if ia < ib else (va > vb)
    return jnp.logical_or(va > vb, jnp.logical_and(va == vb, ia < ib))


def _pick(c, a, b):
    return tuple(jnp.where(c, x, y) for x, y in zip(a, b))


def _ordered_pair(a, b):
    c = _before(a, b)
    return _pick(c, a, b), _pick(c, b, a)


def _bitonic_merge(xs):
    n = len(xs)
    if n == 1:
        return list(xs)
    h = n // 2
    xs = list(xs)
    for k in range(h):
        xs[k], xs[k + h] = _ordered_pair(xs[k], xs[k + h])
    return _bitonic_merge(xs[:h]) + _bitonic_merge(xs[h:])


def _bitonic_sort(xs):
    n = len(xs)
    if n == 1:
        return list(xs)
    return _bitonic_merge(_bitonic_sort(xs[:n // 2]) + _bitonic_sort(xs[n // 2:])[::-1])


def _top_keys(s_ref):
    k = PEER_TOPK
    best = None
    for g in range(PEER_NKEYS // k):
        grp = _bitonic_sort([(s_ref[_key_rows(g * k + a), :], float(g * k + a)) for a in range(k)])
        grp = [(v, jnp.full((SUBLANES, LANES), i, F32) if isinstance(i, float) else i) for v, i in grp]
        if best is None:
            best = grp
        else:
            best = _bitonic_merge([_pick(_before(best[j], grp[k - 1 - j]), best[j], grp[k - 1 - j])
                                   for j in range(k)])
    return [v for v, _ in best], [i for _, i in best]


def _to_token_lanes(src_ref, dst_ref):
    for g in range(SUBLANES):
        dst_ref[g] = src_ref[pl.ds(g, PEER_NKEYS, stride=SUBLANES), :].astype(dst_ref.dtype)


def _route_body(hn_ref, wq_ref, keys_ref, r2_ref, e2_ref, n1_ref, c1_ref, s1_s, s2_s, rk2_s, n1_s):
    q = jnp.dot(hn_ref[...], wq_ref[...], preferred_element_type=F32)
    for c, s_s in enumerate((s1_s, s2_s)):
        qc = q[:, c * PEER_HALF:(c + 1) * PEER_HALF].astype(BF16)
        st = lax.dot_general(keys_ref[c], qc, NT, preferred_element_type=F32)
        for g in range(SUBLANES):
            s_s[pl.ds(g, PEER_NKEYS, stride=SUBLANES), :] = st[:, g * LANES:(g + 1) * LANES]
    sc1, si1 = _top_keys(s1_s)
    sc2, si2 = _top_keys(s2_s)
    e1 = [jnp.exp(sc1[k] - sc1[0]) for k in range(PEER_TOPK)]
    e2 = [jnp.exp(sc2[k] - sc2[0]) for k in range(PEER_TOPK)]

    cand = [sc1[a] + sc2[b] for (a, b) in _CELLS]
    nc = len(_CELLS)
    before = [jnp.zeros((SUBLANES, LANES), F32) for _ in range(nc)]
    for x in range(nc):
        for y in range(x + 1, nc):
            ax, bx = _CELLS[x]
            ay, by = _CELLS[y]
            if ax <= ay and bx <= by:
                before[y] = before[y] + 1.0
                continue
            bt = jnp.where(cand[x] >= cand[y], 1.0, 0.0)
            before[y] = before[y] + bt
            before[x] = before[x] + (1.0 - bt)
    zero = jnp.zeros((SUBLANES, LANES), F32)
    ncol = [zero] * PEER_TOPK
    zsum = zero
    for x, (a, b) in enumerate(_CELLS):
        sel = jnp.where(before[x] < float(PEER_TOPK), 1.0, 0.0)
        ncol[a] = ncol[a] + sel
        zsum = zsum + sel * (e1[a] * e2[b])
    inv_z = 1.0 / zsum

    def dense(i, carry):
        rows = _key_rows(i)
        key = lax.convert_element_type(i, F32)
        n1 = zero
        rk2 = jnp.full((SUBLANES, LANES), float(PEER_TOPK), F32)
        for k in range(PEER_TOPK):
            n1 = jnp.where(si1[k] == key, ncol[k], n1)
            rk2 = jnp.where(si2[k] == key, float(k), rk2)
        n1_s[rows, :] = n1
        rk2_s[rows, :] = rk2
        s1_s[rows, :] = jnp.exp(s1_s[rows, :] - sc1[0]) * inv_z
        s2_s[rows, :] = jnp.exp(s2_s[rows, :] - sc2[0])
        return carry

    lax.fori_loop(0, PEER_NKEYS, dense, 0, unroll=4)
    _to_token_lanes(rk2_s, r2_ref)
    _to_token_lanes(s2_s, e2_ref)
    _to_token_lanes(n1_s, n1_ref)
    _to_token_lanes(s1_s, c1_ref)


def _route(hn, w_q, keys):
    T = hn.shape[0]
    tm = ROUTE_TM
    out_sds = lambda dt: jax.ShapeDtypeStruct((PEER_HEADS, T // LANES, PEER_NKEYS, LANES), dt)
    out_spec = pl.BlockSpec((None, SUBLANES, PEER_NKEYS, LANES), lambda i, h: (h, i, 0, 0))
    km = pltpu.VMEM((PEER_NKEYS * SUBLANES, LANES), F32)
    return pl.pallas_call(
        _route_body,
        out_shape=(out_sds(F32),) * 4,
        grid=(T // tm, PEER_HEADS),
        in_specs=[pl.BlockSpec((tm, D_MODEL), lambda i, h: (i, 0)),
                  pl.BlockSpec((D_MODEL, 2 * PEER_HALF), lambda i, h: (0, h)),
                  pl.BlockSpec((None, 2, PEER_NKEYS, PEER_HALF), lambda i, h: (h, 0, 0, 0))],
        out_specs=(out_spec,) * 4,
        scratch_shapes=[km] * 4,
        compiler_params=_cparams(2),
        name="route",
    )(hn, w_q, keys)


PEER_TE = 512
PEER_QROWS = 32
INV_SQRT2 = 0.7071067811865476


def _peer_tile(a_prev, a_next, jprev, live, hn_ref, u_ref, vt_ref, r2_ref, e2_ref, n1_ref, c1_ref, o_ref, g_scr):
    n_il = PEER_TE // PEER_NKEYS
    packed = 2 * SUBLANES
    reps = PEER_QROWS // packed
    n_qd = PEER_NKEYS // PEER_QROWS
    half = 0.5 * live
    a_next[...] = lax.dot_general(u_ref[...], hn_ref[...], NT, preferred_element_type=F32)
    for il0 in range(0, n_il, 2):
        ils = (il0, il0 + 1)
        for cb in range(g_scr.shape[1] // LANES):
            lanes = slice(cb * LANES, (cb + 1) * LANES)
            w = {(il, qd): None for il in ils for qd in range(n_qd)}
            for h in range(PEER_HEADS):
                row = {}
                for il in ils:
                    key = pl.ds(jprev * n_il + il, packed, stride=0)
                    row[il] = (jnp.concatenate([n1_ref[h, cb, key, :].astype(BF16)] * reps, axis=0),
                               jnp.concatenate([c1_ref[h, cb, key, :].astype(BF16)] * reps, axis=0))
                for qd in range(n_qd):
                    rows = slice(qd * PEER_QROWS, (qd + 1) * PEER_QROWS)
                    r2 = r2_ref[h, cb, rows, :]
                    e2 = e2_ref[h, cb, rows, :]
                    for il in ils:
                        t = jnp.where(r2 < row[il][0], e2, jnp.zeros_like(e2)) * row[il][1]
                        w[il, qd] = t if w[il, qd] is None else w[il, qd] + t
            for il in ils:
                for qd in range(n_qd):
                    er = slice(il * PEER_NKEYS + qd * PEER_QROWS, il * PEER_NKEYS + (qd + 1) * PEER_QROWS)
                    a = a_prev[er, lanes]
                    act = half * a * (1.0 + lax.erf(a * INV_SQRT2))
                    g_scr[er, lanes] = act.astype(BF16) * w[il, qd]
    o_ref[...] += jnp.dot(vt_ref[...], g_scr[...], preferred_element_type=F32)


def _peer_body(hn_ref, u_ref, vt_ref, r2_ref, e2_ref, n1_ref, c1_ref, o_ref, a0, a1, g_scr, r2_b, e2_b):
    i = pl.program_id(0)
    j = pl.program_id(1)

    @pl.when(jnp.logical_and(i == 0, j == 0))
    def _():
        a1[...] = jnp.zeros_like(a1)

    @pl.when(j == 0)
    def _():
        o_ref[...] = jnp.zeros_like(o_ref)
        r2_b[...] = r2_ref[...].astype(BF16)
        e2_b[...] = e2_ref[...].astype(BF16)

    live = jnp.where(j > 0, 1.0, 0.0).astype(F32)
    jprev = jnp.maximum(j - 1, 0)
    tile = functools.partial(_peer_tile, jprev=jprev, live=live, hn_ref=hn_ref, u_ref=u_ref, vt_ref=vt_ref,
                             r2_ref=r2_b, e2_ref=e2_b, n1_ref=n1_ref, c1_ref=c1_ref, o_ref=o_ref, g_scr=g_scr)

    @pl.when(j % 2 == 0)
    def _():
        tile(a1, a0)

    @pl.when(j % 2 == 1)
    def _():
        tile(a0, a1)


def _peer(hn, u, vt, r2, e2, n1, c1, tm=512):
    T = hn.shape[0]
    te = PEER_TE
    nj = PEER_EXPERTS // te
    rspec = pl.BlockSpec((PEER_HEADS, tm // LANES, PEER_NKEYS, LANES), lambda i, j: (0, i, 0, 0))
    return pl.pallas_call(
        _peer_body,
        out_shape=jax.ShapeDtypeStruct((D_MODEL, T), F32),
        grid=(T // tm, nj + 1),
        in_specs=[pl.BlockSpec((tm, D_MODEL), lambda i, j: (i, 0)),
                  pl.BlockSpec((te, D_MODEL), lambda i, j: (jnp.minimum(j, nj - 1), 0)),
                  pl.BlockSpec((None, D_MODEL, te), lambda i, j: (jnp.maximum(j - 1, 0), 0, 0)),
                  rspec, rspec, rspec, rspec],
        out_specs=pl.BlockSpec((D_MODEL, tm), lambda i, j: (0, i)),
        scratch_shapes=([pltpu.VMEM((te, tm), F32), pltpu.VMEM((te, tm), F32), pltpu.VMEM((te, tm), BF16)]
                        + [pltpu.VMEM((PEER_HEADS, tm // LANES, PEER_NKEYS, LANES), BF16)] * 2),
        compiler_params=_cparams(2),
        name="peer",
    )(hn, u, vt, r2, e2, n1, c1)


def _final_body(h_ref, pt_ref, g_ref, o_ref):
    o_ref[...] = _rms(h_ref[...] + pt_ref[...].T, g_ref[...])


def _final(h, peer_t, g, tm=512):
    T = h.shape[0]
    return pl.pallas_call(
        _final_body,
        out_shape=jax.ShapeDtypeStruct((T, D_MODEL), F32),
        grid=(T // tm,),
        in_specs=[pl.BlockSpec((tm, D_MODEL), lambda i: (i, 0)),
                  pl.BlockSpec((D_MODEL, tm), lambda i: (0, i)),
                  pl.BlockSpec((1, D_MODEL), lambda i: (0, 0))],
        out_specs=pl.BlockSpec((tm, D_MODEL), lambda i: (i, 0)),
        compiler_params=_cparams(1),
        name="final",
    )(h, peer_t, g)


def _layer(x, norm1_g, w_in, up_f, bias_f, up_b, bias_b, gla_norm_g, att_norm_g, w_out, norm2_g,
           w_q, sub_keys, peer_u, peer_v):
    B, S, D = x.shape
    T = B * S
    x2 = x.reshape(T, D)
    gd0 = 2 * GLA_QK + 2 * GLA_V
    w_main = jnp.concatenate([w_in[:, :gd0], w_in[:, gd0 + GD_W:]], axis=1).astype(BF16)
    w_gd = w_in[:, gd0:gd0 + GD_W].astype(BF16)
    main, gd = _inproj(x2, norm1_g.reshape(1, D), w_main, w_gd)
    main3 = main.reshape(B, S, MAIN_W)
    o_f, o_b = _gla(main3, gd.reshape(B, S, GD_W), up_f, bias_f.reshape(1, GLA_QK),
                    up_b, bias_b.reshape(1, GLA_QK))
    slopes = jnp.asarray((2.0 ** (-8.0 * np.arange(1, ATT_HEADS + 1) / ATT_HEADS)).astype(np.float32))
    att = _attn(main3, slopes)
    h, hn = _mix(o_f.reshape(T, GLA_V), o_b.reshape(T, GLA_V), main, att.reshape(T, ATT_W), x2,
                 gla_norm_g.reshape(1, GLA_DV), att_norm_g.reshape(1, ATT_W), w_out.astype(BF16),
                 norm2_g.reshape(1, D))
    r2, e2, n1, c1 = _route(hn, w_q.astype(BF16), sub_keys.astype(BF16))
    vt = peer_v.astype(BF16).reshape(PEER_EXPERTS // PEER_TE, PEER_TE, D).transpose(0, 2, 1)
    peer_t = _peer(hn, peer_u.astype(BF16), vt, r2, e2, n1, c1)
    return h, peer_t


def kernel(x, norm1_g, w_in, gla_gate_up_f, gla_gate_bias_f, gla_gate_up_b, gla_gate_bias_b, gla_norm_g,
           att_norm_g, w_out, norm2_g, peer_w_q, peer_sub_keys, peer_u, peer_v, final_norm_g):
    B, S, D = x.shape
    assert norm1_g.shape[0] == 1, "single trunk layer: the final norm is fused with the layer's last residual"
    h, peer_t = _layer(x, norm1_g[0], w_in[0], gla_gate_up_f[0], gla_gate_bias_f[0], gla_gate_up_b[0],
                       gla_gate_bias_b[0], gla_norm_g[0], att_norm_g[0], w_out[0], norm2_g[0],
                       peer_w_q[0], peer_sub_keys[0], peer_u[0], peer_v[0])
    return _final(h, peer_t, final_norm_g.reshape(1, D)).reshape(B, S, D)
```

```python
import functools

import numpy as np
import jax
import jax.numpy as jnp
from jax import lax
from jax.experimental import pallas as pl
from jax.experimental.pallas import tpu as pltpu

F32 = jnp.float32
BF16 = jnp.bfloat16
I32 = jnp.int32

D_MODEL = 2048
GLA_HEADS = 4
GLA_DK = 128
GLA_DV = 256
GLA_RANK = 16
GLA_TAU = 16.0
GLA_CHUNK = 64
GLA_QK = GLA_HEADS * GLA_DK
GLA_V = GLA_HEADS * GLA_DV
ATT_HEADS = 16
ATT_HD = 64
ATT_W = ATT_HEADS * ATT_HD
ATT_DILATIONS = (1, 4, 16)
ATT_RADIUS = 64
PEER_NKEYS = 128
PEER_HEADS = 8
PEER_TOPK = 16
PEER_EXPERTS = PEER_NKEYS * PEER_NKEYS
PEER_HALF = 128
EPS = 1e-6
NEG = -1e30
MAIN_W = 2 * GLA_QK + 2 * GLA_V + 3 * ATT_W
GD_W = 2 * GLA_RANK

LANES = 128
SUBLANES = 8
VMEM_LIMIT = 56 * 1024 * 1024

NT = (((1,), (1,)), ((), ()))
TN = (((0,), (0,)), ((), ()))


def _cparams(n_axes):
    return pltpu.CompilerParams(dimension_semantics=("arbitrary",) * n_axes, vmem_limit_bytes=VMEM_LIMIT)


def _rms(x, g):
    ms = jnp.mean(x * x, axis=-1, keepdims=True)
    return x * lax.rsqrt(ms + EPS) * g


def _inproj_body(x_ref, g_ref, w_ref, wgd_ref, o_ref, gd_ref, xn_ref):
    @pl.when(pl.program_id(1) == 0)
    def _():
        xn = _rms(x_ref[...], g_ref[...]).astype(BF16)
        xn_ref[...] = xn
        gd_ref[...] = jnp.dot(xn, wgd_ref[...], preferred_element_type=F32)

    o_ref[...] = jnp.dot(xn_ref[...], w_ref[...], preferred_element_type=F32)


def _inproj(x2, g, w_main, w_gd, tm=1024, tn=1536):
    T = x2.shape[0]
    return pl.pallas_call(
        _inproj_body,
        out_shape=(jax.ShapeDtypeStruct((T, MAIN_W), F32), jax.ShapeDtypeStruct((T, GD_W), F32)),
        grid=(T // tm, MAIN_W // tn),
        in_specs=[pl.BlockSpec((tm, D_MODEL), lambda i, j: (i, 0)),
                  pl.BlockSpec((1, D_MODEL), lambda i, j: (0, 0)),
                  pl.BlockSpec((D_MODEL, tn), lambda i, j: (0, j)),
                  pl.BlockSpec((D_MODEL, GD_W), lambda i, j: (0, 0))],
        out_specs=(pl.BlockSpec((tm, tn), lambda i, j: (i, j)),
                   pl.BlockSpec((tm, GD_W), lambda i, j: (i, 0))),
        scratch_shapes=[pltpu.VMEM((tm, D_MODEL), BF16)],
        compiler_params=_cparams(2),
        name="inproj",
    )(x2, g, w_main, w_gd)


def _log_gate(gd, up_ref, bias_ref):
    z = jnp.dot(gd.astype(BF16), up_ref[...].astype(BF16), preferred_element_type=F32) + bias_ref[...]
    return (jnp.minimum(z, 0.0) - jnp.log1p(jnp.exp(-jnp.abs(z)))) * (1.0 / GLA_TAU)


def _chunk_cumsum(la, tri_mask, b_ref, nchunk):
    C = GLA_CHUNK
    hi = la.astype(BF16)
    lo = (la - hi.astype(F32)).astype(BF16)
    cols = []
    for c in range(nchunk):
        cols += [hi[c * C:(c + 1) * C, :], lo[c * C:(c + 1) * C, :]]
    tri = jnp.where(tri_mask, 1.0, 0.0).astype(BF16)
    both = jnp.dot(tri, jnp.concatenate(cols, axis=1), preferred_element_type=F32)
    for c in range(nchunk):
        b_ref[c * C:(c + 1) * C, :] = (both[:, 2 * c * GLA_DK:(2 * c + 1) * GLA_DK]
                                       + both[:, (2 * c + 1) * GLA_DK:(2 * c + 2) * GLA_DK])


def _gla_chunk(q_ref, k_ref, v_ref, b_ref, st_ref, o_ref, r0, tri_mask, tot_row):
    C = GLA_CHUNK
    sl = pl.ds(pl.multiple_of(r0, C), C)
    b = b_ref[sl, :]
    btot = b[tot_row:tot_row + 1, :]
    q = q_ref[sl, :]
    k = k_ref[sl, :]
    v = v_ref[sl, :].astype(BF16)
    qd = (q * (jnp.exp(b) * (GLA_DK ** -0.5))).astype(BF16)
    kd = (k * jnp.exp(-b)).astype(BF16)
    kr = (k * jnp.exp(btot - b)).astype(BF16)
    att = lax.dot_general(qd, kd, NT, preferred_element_type=F32)
    att = jnp.where(tri_mask, att, 0.0).astype(BF16)
    st = st_ref[...]
    o = (jnp.dot(att, v, preferred_element_type=F32)
         + lax.dot_general(qd, st.astype(BF16), NT, preferred_element_type=F32))
    o_ref[sl, :] = o
    st_ref[...] = jnp.exp(btot) * st + lax.dot_general(v, kr, TN, preferred_element_type=F32)


def _gla_body(qf, kf, vf, gdf, qb, kb, vb, gdb, upf, bsf, upb, bsb, of_ref, ob_ref,
              stf, stb, laf, lab, *, nchunk):
    @pl.when(pl.program_id(2) == 0)
    def _():
        stf[...] = jnp.zeros_like(stf)
        stb[...] = jnp.zeros_like(stb)

    C = GLA_CHUNK
    row = lax.broadcasted_iota(I32, (C, C), 0)
    col = lax.broadcasted_iota(I32, (C, C), 1)
    lower = row >= col
    upper = col >= row
    _chunk_cumsum(_log_gate(gdf[:, 0:GLA_RANK], upf, bsf), lower, laf, nchunk)
    _chunk_cumsum(_log_gate(gdb[:, GLA_RANK:2 * GLA_RANK], upb, bsb), upper, lab, nchunk)

    def step(c, carry):
        _gla_chunk(qf, kf, vf, laf, stf, of_ref, c * C, lower, C - 1)
        _gla_chunk(qb, kb, vb, lab, stb, ob_ref, (nchunk - 1 - c) * C, upper, 0)
        return carry

    lax.fori_loop(0, nchunk, step, 0, unroll=16)


def _gla(main3, gd3, up_f, bias_f, up_b, bias_b, lb=1024):
    B, S, _ = main3.shape
    nb = S // lb
    kq = GLA_QK // GLA_DK
    kv = (2 * GLA_QK) // GLA_DV
    fwd = lambda b, h, i: (b, i, h)
    blk = lambda c0: (lambda b, h, i: (b, i, c0 + h))
    rblk = lambda c0: (lambda b, h, i: (b, nb - 1 - i, c0 + h))
    sq = lambda m: pl.BlockSpec((None, lb, GLA_DK), m)
    sv = lambda m: pl.BlockSpec((None, lb, GLA_DV), m)
    sg = lambda m: pl.BlockSpec((None, lb, GD_W), m)
    up_spec = pl.BlockSpec((GLA_RANK, GLA_DK), lambda b, h, i: (0, h))
    bias_spec = pl.BlockSpec((1, GLA_DK), lambda b, h, i: (0, h))
    out_sds = jax.ShapeDtypeStruct((B, S, GLA_V), F32)
    return pl.pallas_call(
        functools.partial(_gla_body, nchunk=lb // GLA_CHUNK),
        out_shape=(out_sds, out_sds),
        grid=(B, GLA_HEADS, nb),
        in_specs=[sq(blk(0)), sq(blk(kq)), sv(blk(kv)), sg(lambda b, h, i: (b, i, 0)),
                  sq(rblk(0)), sq(rblk(kq)), sv(rblk(kv)), sg(lambda b, h, i: (b, nb - 1 - i, 0)),
                  up_spec, bias_spec, up_spec, bias_spec],
        out_specs=(pl.BlockSpec((None, lb, GLA_DV), fwd),
                   pl.BlockSpec((None, lb, GLA_DV), lambda b, h, i: (b, nb - 1 - i, h))),
        scratch_shapes=[pltpu.VMEM((GLA_DV, GLA_DK), F32), pltpu.VMEM((GLA_DV, GLA_DK), F32),
                        pltpu.VMEM((lb, GLA_DK), F32), pltpu.VMEM((lb, GLA_DK), F32)],
        compiler_params=_cparams(3),
        name="gla",
    )(main3, main3, main3, gd3, main3, main3, main3, gd3, up_f, bias_f, up_b, bias_b)


ATT_TQ = 128
ATT_WIN = 256


ATT_GROUP = 32


ATT_FACTOR = 4
assert all(b == a * ATT_FACTOR for a, b in zip(ATT_DILATIONS[:-1], ATT_DILATIONS[1:])) and ATT_DILATIONS[0] == 1


def _split_classes(src, dst, d, n):
    f = ATT_FACTOR
    sub = n // (d * f)
    for r in range(d):
        for a in range(f):
            dst[pl.ds((d * a + r) * sub, sub), :] = src[pl.ds(r * (n // d) + a, sub, stride=f), :]


def _merge_classes(src, dst, d, n):
    f = ATT_FACTOR
    sub = n // (d * f)
    for r in range(d):
        for a in range(f):
            dst[pl.ds(r * (n // d) + a, sub, stride=f), :] = src[pl.ds((d * a + r) * sub, sub), :]


def _attn_body(slopes_ref, q_ref, k_ref, v_ref, o_ref, *scr, seq):
    pair = pl.program_id(1)
    lane = lax.broadcasted_iota(I32, (1, LANES), 1)
    head0 = lane < ATT_HD
    ii = lax.broadcasted_iota(I32, (ATT_TQ, ATT_WIN), 0)
    jj = lax.broadcasted_iota(I32, (ATT_TQ, ATT_WIN), 1)
    slope = (slopes_ref[2 * pair], slopes_ref[2 * pair + 1])
    npat = len(ATT_DILATIONS)
    copies = [(q_ref, k_ref, v_ref)] + [scr[3 * i:3 * i + 3] for i in range(npat - 1)]
    stats = [scr[3 * (npat - 1):3 * npat], scr[3 * npat:3 * (npat + 1)]]
    bias_s = scr[3 * (npat + 1)]
    for pi in range(1, npat):
        for x in range(3):
            _split_classes(copies[pi - 1][x], copies[pi][x], ATT_DILATIONS[pi - 1], seq)

    for step, pi in enumerate(reversed(range(npat))):
        d = ATT_DILATIONS[pi]
        L = seq // d
        nblk = L // ATT_TQ
        first = step == 0
        q_ref, k_ref, v_ref = copies[pi]
        if not first:
            for x in range(3):
                _merge_classes(stats[(step - 1) % 2][x], stats[step % 2][x], d, seq)
        m_s, l_s, acc_s = stats[step % 2]
        for case in range(3):
            ad = jnp.abs(jj - ii - case * ATT_RADIUS)
            for hh in range(2):
                bias_s[case, hh] = jnp.where(ad <= ATT_RADIUS, -slope[hh] * (ad * d).astype(F32), NEG)

        def group(gi, carry, L=L, nblk=nblk, first=first, q_ref=q_ref, k_ref=k_ref, v_ref=v_ref,
                  m_s=m_s, l_s=l_s, acc_s=acc_s):
            loaded = []
            for u in range(ATT_GROUP):
                t = gi * ATT_GROUP + u
                base = (t // nblk) * L
                q0 = (t % nblk) * ATT_TQ
                start = jnp.clip(q0 - ATT_RADIUS, 0, L - ATT_WIN)
                case = (q0 - start) // ATT_RADIUS
                qsl = pl.ds(pl.multiple_of(base + q0, ATT_TQ), ATT_TQ)
                ksl = pl.ds(pl.multiple_of(base + start, ATT_RADIUS), ATT_WIN)
                q = q_ref[qsl, :] * (ATT_HD ** -0.5)
                k = k_ref[ksl, :].astype(BF16)
                v = v_ref[ksl, :].astype(BF16)
                run = None if first else (m_s[qsl, :], l_s[qsl, :], acc_s[qsl, :])
                loaded.append((qsl, case, q, k, v, run))
            results = []
            for qsl, case, q, k, v, run in loaded:
                m_new, lsum, pv = [], [], []
                for hh in range(2):
                    hm = head0 if hh == 0 else jnp.logical_not(head0)
                    qh = jnp.where(hm, q, 0.0).astype(BF16)
                    s = lax.dot_general(qh, k, NT, preferred_element_type=F32) + bias_s[case, hh]
                    mh = jnp.max(s, axis=-1, keepdims=True)
                    p = jnp.exp(s - mh)
                    m_new.append(mh)
                    lsum.append(jnp.sum(p, axis=-1, keepdims=True))
                    pv.append(jnp.dot(p.astype(BF16), v, preferred_element_type=F32))
                m_full = jnp.where(head0, m_new[0], m_new[1])
                l_blk = jnp.where(head0, lsum[0], lsum[1])
                pv_blk = jnp.where(head0, pv[0], pv[1])
                if not first:
                    m_blk = m_full
                    m_full = jnp.maximum(run[0], m_blk)
                    a_run = jnp.exp(run[0] - m_full)
                    a_blk = jnp.exp(m_blk - m_full)
                    l_blk = a_run * run[1] + a_blk * l_blk
                    pv_blk = a_run * run[2] + a_blk * pv_blk
                results.append((qsl, m_full, l_blk, pv_blk))
            for qsl, m_full, l_blk, pv_blk in results:
                m_s[qsl, :] = m_full
                l_s[qsl, :] = l_blk
                acc_s[qsl, :] = pv_blk
            return carry

        lax.fori_loop(0, d * nblk // ATT_GROUP, group, 0)

    o_ref[...] = acc_s[...] / l_s[...]


def _attn(main3, slopes):
    B, S, _ = main3.shape
    c0 = (2 * GLA_QK + 2 * GLA_V) // LANES
    npair = ATT_HEADS // 2
    spec = lambda off: pl.BlockSpec((None, S, LANES), lambda b, p: (b, 0, c0 + off + p))
    return pl.pallas_call(
        functools.partial(_attn_body, seq=S),
        out_shape=jax.ShapeDtypeStruct((B, S, ATT_W), F32),
        grid=(B, npair),
        in_specs=[pl.BlockSpec(memory_space=pltpu.SMEM), spec(0), spec(npair), spec(2 * npair)],
        out_specs=pl.BlockSpec((None, S, LANES), lambda b, p: (b, 0, p)),
        scratch_shapes=([pltpu.VMEM((S, LANES), F32)] * (3 * (len(ATT_DILATIONS) + 1))
                        + [pltpu.VMEM((3, 2, ATT_TQ, ATT_WIN), F32)]),
        compiler_params=_cparams(2),
        name="attn",
    )(slopes, main3, main3, main3)


def _mix_body(of_ref, ob_ref, gr_ref, att_ref, x_ref, gg_ref, ag_ref, wo_ref, n2_ref, h_ref, hn_ref):
    o = of_ref[...] + ob_ref[...]
    parts = [_rms(o[:, h * GLA_DV:(h + 1) * GLA_DV], gg_ref[...]) for h in range(GLA_HEADS)]
    on = jnp.concatenate(parts, axis=-1)
    gr = gr_ref[...]
    gla = on * (gr / (1.0 + jnp.exp(-gr)))
    an = _rms(att_ref[...], ag_ref[...])
    y = (jnp.dot(gla.astype(BF16), wo_ref[0:GLA_V, :], preferred_element_type=F32)
         + jnp.dot(an.astype(BF16), wo_ref[GLA_V:GLA_V + ATT_W, :], preferred_element_type=F32))
    h = x_ref[...] + y
    h_ref[...] = h
    hn_ref[...] = _rms(h, n2_ref[...]).astype(BF16)


def _mix(o_f, o_b, main2, att2, x2, gg, ag, w_out, n2g, tm=512):
    T = x2.shape[0]
    gr_blk = (2 * GLA_QK + GLA_V) // GLA_V
    row = lambda i: (i, 0)
    const = lambda i: (0, 0)
    return pl.pallas_call(
        _mix_body,
        out_shape=(jax.ShapeDtypeStruct((T, D_MODEL), F32), jax.ShapeDtypeStruct((T, D_MODEL), BF16)),
        grid=(T // tm,),
        in_specs=[pl.BlockSpec((tm, GLA_V), row), pl.BlockSpec((tm, GLA_V), row),
                  pl.BlockSpec((tm, GLA_V), lambda i: (i, gr_blk)),
                  pl.BlockSpec((tm, ATT_W), row), pl.BlockSpec((tm, D_MODEL), row),
                  pl.BlockSpec((1, GLA_DV), const), pl.BlockSpec((1, ATT_W), const),
                  pl.BlockSpec((D_MODEL, D_MODEL), const), pl.BlockSpec((1, D_MODEL), const)],
        out_specs=(pl.BlockSpec((tm, D_MODEL), row), pl.BlockSpec((tm, D_MODEL), row)),
        compiler_params=_cparams(1),
        name="mix",
    )(o_f, o_b, main2, att2, x2, gg, ag, w_out, n2g)


ROUTE_TM = SUBLANES * LANES
_CELLS = [(a, b) for a in range(PEER_TOPK) for b in range(PEER_TOPK) if (a + 1) * (b + 1) <= PEER_TOPK]


def _key_rows(i):
    return pl.ds(i * SUBLANES, SUBLANES)


def _before(a, b):
    (va, ia), (vb, ib) = a, b
    if isinstance(ia, float) and isinstance(ib, float):
        return (va >= vb) if ia < ib else (va > vb)
    return jnp.logical_or(va > vb, jnp.logical_and(va == vb, ia < ib))


def _pick(c, a, b):
    return tuple(jnp.where(c, x, y) for x, y in zip(a, b))


def _ordered_pair(a, b):
    c = _before(a, b)
    return _pick(c, a, b), _pick(c, b, a)


def _bitonic_merge(xs):
    n = len(xs)
    if n == 1:
        return list(xs)
    h = n // 2
    xs = list(xs)
    for k in range(h):
        xs[k], xs[k + h] = _ordered_pair(xs[k], xs[k + h])
    return _bitonic_merge(xs[:h]) + _bitonic_merge(xs[h:])


def _bitonic_sort(xs):
    n = len(xs)
    if n == 1:
        return list(xs)
    return _bitonic_merge(_bitonic_sort(xs[:n // 2]) + _bitonic_sort(xs[n // 2:])[::-1])


def _top_keys(s_ref):
    k = PEER_TOPK
    best = None
    for g in range(PEER_NKEYS // k):
        grp = _bitonic_sort([(s_ref[_key_rows(g * k + a), :], float(g * k + a)) for a in range(k)])
        grp = [(v, jnp.full((SUBLANES, LANES), i, F32) if isinstance(i, float) else i) for v, i in grp]
        if best is None:
            best = grp
        else:
            best = _bitonic_merge([_pick(_before(best[j], grp[k - 1 - j]), best[j], grp[k - 1 - j])
                                   for j in range(k)])
    return [v for v, _ in best], [i for _, i in best]


def _to_token_lanes(src_ref, dst_ref):
    for g in range(SUBLANES):
        dst_ref[g] = src_ref[pl.ds(g, PEER_NKEYS, stride=SUBLANES), :].astype(dst_ref.dtype)


def _route_body(hn_ref, wq_ref, keys_ref, r2_ref, e2_ref, n1_ref, c1_ref, s1_s, s2_s, rk2_s, n1_s):
    q = jnp.dot(hn_ref[...], wq_ref[...], preferred_element_type=F32)
    for c, s_s in enumerate((s1_s, s2_s)):
        qc = q[:, c * PEER_HALF:(c + 1) * PEER_HALF].astype(BF16)
        st = lax.dot_general(keys_ref[c], qc, NT, preferred_element_type=F32)
        for g in range(SUBLANES):
            s_s[pl.ds(g, PEER_NKEYS, stride=SUBLANES), :] = st[:, g * LANES:(g + 1) * LANES]
    sc1, si1 = _top_keys(s1_s)
    sc2, si2 = _top_keys(s2_s)
    e1 = [jnp.exp(sc1[k] - sc1[0]) for k in range(PEER_TOPK)]
    e2 = [jnp.exp(sc2[k] - sc2[0]) for k in range(PEER_TOPK)]

    cand = [sc1[a] + sc2[b] for (a, b) in _CELLS]
    nc = len(_CELLS)
    before = [jnp.zeros((SUBLANES, LANES), F32) for _ in range(nc)]
    for x in range(nc):
        for y in range(x + 1, nc):
            ax, bx = _CELLS[x]
            ay, by = _CELLS[y]
            if ax <= ay and bx <= by:
                before[y] = before[y] + 1.0
                continue
            bt = jnp.where(cand[x] >= cand[y], 1.0, 0.0)
            before[y] = before[y] + bt
            before[x] = before[x] + (1.0 - bt)
    zero = jnp.zeros((SUBLANES, LANES), F32)
    ncol = [zero] * PEER_TOPK
    zsum = zero
    for x, (a, b) in enumerate(_CELLS):
        sel = jnp.where(before[x] < float(PEER_TOPK), 1.0, 0.0)
        ncol[a] = ncol[a] + sel
        zsum = zsum + sel * (e1[a] * e2[b])
    inv_z = 1.0 / zsum

    def dense(i, carry):
        rows = _key_rows(i)
        key = lax.convert_element_type(i, F32)
        n1 = zero
        rk2 = jnp.full((SUBLANES, LANES), float(PEER_TOPK), F32)
        for k in range(PEER_TOPK):
            n1 = jnp.where(si1[k] == key, ncol[k], n1)
            rk2 = jnp.where(si2[k] == key, float(k), rk2)
        n1_s[rows, :] = n1
        rk2_s[rows, :] = rk2
        s1_s[rows, :] = jnp.exp(s1_s[rows, :] - sc1[0]) * inv_z
        s2_s[rows, :] = jnp.exp(s2_s[rows, :] - sc2[0])
        return carry

    lax.fori_loop(0, PEER_NKEYS, dense, 0, unroll=4)
    _to_token_lanes(rk2_s, r2_ref)
    _to_token_lanes(s2_s, e2_ref)
    _to_token_lanes(n1_s, n1_ref)
    _to_token_lanes(s1_s, c1_ref)


def _route(hn, w_q, keys):
    T = hn.shape[0]
    tm = ROUTE_TM
    out_sds = lambda dt: jax.ShapeDtypeStruct((PEER_HEADS, T // LANES, PEER_NKEYS, LANES), dt)
    out_spec = pl.BlockSpec((None, SUBLANES, PEER_NKEYS, LANES), lambda i, h: (h, i, 0, 0))
    km = pltpu.VMEM((PEER_NKEYS * SUBLANES, LANES), F32)
    return pl.pallas_call(
        _route_body,
        out_shape=(out_sds(F32),) * 4,
        grid=(T // tm, PEER_HEADS),
        in_specs=[pl.BlockSpec((tm, D_MODEL), lambda i, h: (i, 0)),
                  pl.BlockSpec((D_MODEL, 2 * PEER_HALF), lambda i, h: (0, h)),
                  pl.BlockSpec((None, 2, PEER_NKEYS, PEER_HALF), lambda i, h: (h, 0, 0, 0))],
        out_specs=(out_spec,) * 4,
        scratch_shapes=[km] * 4,
        compiler_params=_cparams(2),
        name="route",
    )(hn, w_q, keys)


PEER_TE = 512
PEER_QROWS = 32
INV_SQRT2 = 0.7071067811865476


def _peer_body(hn_ref, u_ref, vt_ref, r2_ref, e2_ref, n1_ref, c1_ref, o_ref, a_scr, g_scr, r2_b, e2_b):
    j = pl.program_id(1)

    @pl.when(j == 0)
    def _():
        o_ref[...] = jnp.zeros_like(o_ref)
        r2_b[...] = r2_ref[...].astype(BF16)
        e2_b[...] = e2_ref[...].astype(BF16)

    n_il = PEER_TE // PEER_NKEYS
    packed = 2 * SUBLANES
    reps = PEER_QROWS // packed
    n_qd = PEER_NKEYS // PEER_QROWS
    a_scr[...] = lax.dot_general(u_ref[...], hn_ref[...], NT, preferred_element_type=F32)
    for il0 in range(0, n_il, 2):
        ils = (il0, il0 + 1)
        for cb in range(g_scr.shape[1] // LANES):
            lanes = slice(cb * LANES, (cb + 1) * LANES)
            w = {(il, qd): None for il in ils for qd in range(n_qd)}
            for h in range(PEER_HEADS):
                row = {}
                for il in ils:
                    key = pl.ds(j * n_il + il, packed, stride=0)
                    row[il] = (jnp.concatenate([n1_ref[h, cb, key, :].astype(BF16)] * reps, axis=0),
                               jnp.concatenate([c1_ref[h, cb, key, :].astype(BF16)] * reps, axis=0))
                for qd in range(n_qd):
                    rows = slice(qd * PEER_QROWS, (qd + 1) * PEER_QROWS)
                    r2 = r2_b[h, cb, rows, :]
                    e2 = e2_b[h, cb, rows, :]
                    for il in ils:
                        t = jnp.where(r2 < row[il][0], e2, jnp.zeros_like(e2)) * row[il][1]
                        w[il, qd] = t if w[il, qd] is None else w[il, qd] + t
            for il in ils:
                for qd in range(n_qd):
                    er = slice(il * PEER_NKEYS + qd * PEER_QROWS, il * PEER_NKEYS + (qd + 1) * PEER_QROWS)
                    a = a_scr[er, lanes]
                    act = 0.5 * a * (1.0 + lax.erf(a * INV_SQRT2))
                    g_scr[er, lanes] = act.astype(BF16) * w[il, qd]
    o_ref[...] += jnp.dot(vt_ref[...], g_scr[...], preferred_element_type=F32)


def _peer(hn, u, vt, r2, e2, n1, c1, tm=512):
    T = hn.shape[0]
    te = PEER_TE
    nj = PEER_EXPERTS // te
    rspec = pl.BlockSpec((PEER_HEADS, tm // LANES, PEER_NKEYS, LANES), lambda i, j: (0, i, 0, 0))
    return pl.pallas_call(
        _peer_body,
        out_shape=jax.ShapeDtypeStruct((D_MODEL, T), F32),
        grid=(T // tm, nj),
        in_specs=[pl.BlockSpec((tm, D_MODEL), lambda i, j: (i, 0)),
                  pl.BlockSpec((te, D_MODEL), lambda i, j: (j, 0)),
                  pl.BlockSpec((None, D_MODEL, te), lambda i, j: (j, 0, 0)),
                  rspec, rspec, rspec, rspec],
        out_specs=pl.BlockSpec((D_MODEL, tm), lambda i, j: (0, i)),
        scratch_shapes=([pltpu.VMEM((te, tm), F32), pltpu.VMEM((te, tm), BF16)]
                        + [pltpu.VMEM((PEER_HEADS, tm // LANES, PEER_NKEYS, LANES), BF16)] * 2),
        compiler_params=_cparams(2),
        name="peer",
    )(hn, u, vt, r2, e2, n1, c1)


def _final_body(h_ref, pt_ref, g_ref, o_ref):
    o_ref[...] = _rms(h_ref[...] + pt_ref[...].T, g_ref[...])


def _final(h, peer_t, g, tm=512):
    T = h.shape[0]
    return pl.pallas_call(
        _final_body,
        out_shape=jax.ShapeDtypeStruct((T, D_MODEL), F32),
        grid=(T // tm,),
        in_specs=[pl.BlockSpec((tm, D_MODEL), lambda i: (i, 0)),
                  pl.BlockSpec((D_MODEL, tm), lambda i: (0, i)),
                  pl.BlockSpec((1, D_MODEL), lambda i: (0, 0))],
        out_specs=pl.BlockSpec((tm, D_MODEL), lambda i: (i, 0)),
        compiler_params=_cparams(1),
        name="final",
    )(h, peer_t, g)


def _layer(x, norm1_g, w_in, up_f, bias_f, up_b, bias_b, gla_norm_g, att_norm_g, w_out, norm2_g,
           w_q, sub_keys, peer_u, peer_v):
    B, S, D = x.shape
    T = B * S
    x2 = x.reshape(T, D)
    gd0 = 2 * GLA_QK + 2 * GLA_V
    w_main = jnp.concatenate([w_in[:, :gd0], w_in[:, gd0 + GD_W:]], axis=1).astype(BF16)
    w_gd = w_in[:, gd0:gd0 + GD_W].astype(BF16)
    main, gd = _inproj(x2, norm1_g.reshape(1, D), w_main, w_gd)
    main3 = main.reshape(B, S, MAIN_W)
    o_f, o_b = _gla(main3, gd.reshape(B, S, GD_W), up_f, bias_f.reshape(1, GLA_QK),
                    up_b, bias_b.reshape(1, GLA_QK))
    slopes = jnp.asarray((2.0 ** (-8.0 * np.arange(1, ATT_HEADS + 1) / ATT_HEADS)).astype(np.float32))
    att = _attn(main3, slopes)
    h, hn = _mix(o_f.reshape(T, GLA_V), o_b.reshape(T, GLA_V), main, att.reshape(T, ATT_W), x2,
                 gla_norm_g.reshape(1, GLA_DV), att_norm_g.reshape(1, ATT_W), w_out.astype(BF16),
                 norm2_g.reshape(1, D))
    r2, e2, n1, c1 = _route(hn, w_q.astype(BF16), sub_keys.astype(BF16))
    vt = peer_v.astype(BF16).reshape(PEER_EXPERTS // PEER_TE, PEER_TE, D).transpose(0, 2, 1)
    peer_t = _peer(hn, peer_u.astype(BF16), vt, r2, e2, n1, c1)
    return h, peer_t


def kernel(x, norm1_g, w_in, gla_gate_up_f, gla_gate_bias_f, gla_gate_up_b, gla_gate_bias_b, gla_norm_g,
           att_norm_g, w_out, norm2_g, peer_w_q, peer_sub_keys, peer_u, peer_v, final_norm_g):
    B, S, D = x.shape
    assert norm1_g.shape[0] == 1, "single trunk layer: the final norm is fused with the layer's last residual"
    h, peer_t = _layer(x, norm1_g[0], w_in[0], gla_gate_up_f[0], gla_gate_bias_f[0], gla_gate_up_b[0],
                       gla_gate_bias_b[0], gla_norm_g[0], att_norm_g[0], w_out[0], norm2_g[0],
                       peer_w_q[0], peer_sub_keys[0], peer_u[0], peer_v[0])
    return _final(h, peer_t, final_norm_g.reshape(1, D)).reshape(B, S, D)
```

```python
import functools

import numpy as np
import jax
import jax.numpy as jnp
from jax import lax
from jax.experimental import pallas as pl
from jax.experimental.pallas import tpu as pltpu

F32 = jnp.float32
BF16 = jnp.bfloat16
I32 = jnp.int32

D_MODEL = 2048
GLA_HEADS = 4
GLA_DK = 128
GLA_DV = 256
GLA_RANK = 16
GLA_TAU = 16.0
GLA_CHUNK = 64
GLA_QK = GLA_HEADS * GLA_DK
GLA_V = GLA_HEADS * GLA_DV
ATT_HEADS = 16
ATT_HD = 64
ATT_W = ATT_HEADS * ATT_HD
ATT_DILATIONS = (1, 4, 16)
ATT_RADIUS = 64
PEER_NKEYS = 128
PEER_HEADS = 8
PEER_TOPK = 16
PEER_EXPERTS = PEER_NKEYS * PEER_NKEYS
PEER_HALF = 128
EPS = 1e-6
NEG = -1e30
MAIN_W = 2 * GLA_QK + 2 * GLA_V + 3 * ATT_W
GD_W = 2 * GLA_RANK

LANES = 128
SUBLANES = 8
VMEM_LIMIT = 56 * 1024 * 1024

NT = (((1,), (1,)), ((), ()))
TN = (((0,), (0,)), ((), ()))


def _cparams(n_axes):
    return pltpu.CompilerParams(dimension_semantics=("arbitrary",) * n_axes, vmem_limit_bytes=VMEM_LIMIT)


def _rms(x, g):
    ms = jnp.mean(x * x, axis=-1, keepdims=True)
    return x * lax.rsqrt(ms + EPS) * g


def _inproj_body(x_ref, g_ref, w_ref, wgd_ref, o_ref, gd_ref, xn_ref):
    @pl.when(pl.program_id(1) == 0)
    def _():
        xn = _rms(x_ref[...], g_ref[...]).astype(BF16)
        xn_ref[...] = xn
        gd_ref[...] = jnp.dot(xn, wgd_ref[...], preferred_element_type=F32)

    o_ref[...] = jnp.dot(xn_ref[...], w_ref[...], preferred_element_type=F32)


def _inproj(x2, g, w_main, w_gd, tm=1024, tn=1536):
    T = x2.shape[0]
    return pl.pallas_call(
        _inproj_body,
        out_shape=(jax.ShapeDtypeStruct((T, MAIN_W), F32), jax.ShapeDtypeStruct((T, GD_W), F32)),
        grid=(T // tm, MAIN_W // tn),
        in_specs=[pl.BlockSpec((tm, D_MODEL), lambda i, j: (i, 0)),
                  pl.BlockSpec((1, D_MODEL), lambda i, j: (0, 0)),
                  pl.BlockSpec((D_MODEL, tn), lambda i, j: (0, j)),
                  pl.BlockSpec((D_MODEL, GD_W), lambda i, j: (0, 0))],
        out_specs=(pl.BlockSpec((tm, tn), lambda i, j: (i, j)),
                   pl.BlockSpec((tm, GD_W), lambda i, j: (i, 0))),
        scratch_shapes=[pltpu.VMEM((tm, D_MODEL), BF16)],
        compiler_params=_cparams(2),
        name="inproj",
    )(x2, g, w_main, w_gd)


def _log_gate(gd, up_ref, bias_ref):
    z = jnp.dot(gd.astype(BF16), up_ref[...].astype(BF16), preferred_element_type=F32) + bias_ref[...]
    return (jnp.minimum(z, 0.0) - jnp.log1p(jnp.exp(-jnp.abs(z)))) * (1.0 / GLA_TAU)


def _chunk_cumsum(la, tri_mask, b_ref, nchunk):
    C = GLA_CHUNK
    hi = la.astype(BF16)
    lo = (la - hi.astype(F32)).astype(BF16)
    cols = []
    for c in range(nchunk):
        cols += [hi[c * C:(c + 1) * C, :], lo[c * C:(c + 1) * C, :]]
    tri = jnp.where(tri_mask, 1.0, 0.0).astype(BF16)
    both = jnp.dot(tri, jnp.concatenate(cols, axis=1), preferred_element_type=F32)
    for c in range(nchunk):
        b_ref[c * C:(c + 1) * C, :] = (both[:, 2 * c * GLA_DK:(2 * c + 1) * GLA_DK]
                                       + both[:, (2 * c + 1) * GLA_DK:(2 * c + 2) * GLA_DK])


def _gla_chunk(q_ref, k_ref, v_ref, b_ref, st_ref, o_ref, r0, tri_mask, tot_row):
    C = GLA_CHUNK
    sl = pl.ds(pl.multiple_of(r0, C), C)
    b = b_ref[sl, :]
    btot = b[tot_row:tot_row + 1, :]
    q = q_ref[sl, :]
    k = k_ref[sl, :]
    v = v_ref[sl, :].astype(BF16)
    qd = (q * (jnp.exp(b) * (GLA_DK ** -0.5))).astype(BF16)
    kd = (k * jnp.exp(-b)).astype(BF16)
    kr = (k * jnp.exp(btot - b)).astype(BF16)
    att = lax.dot_general(qd, kd, NT, preferred_element_type=F32)
    att = jnp.where(tri_mask, att, 0.0).astype(BF16)
    st = st_ref[...]
    o = (jnp.dot(att, v, preferred_element_type=F32)
         + lax.dot_general(qd, st.astype(BF16), NT, preferred_element_type=F32))
    o_ref[sl, :] = o
    st_ref[...] = jnp.exp(btot) * st + lax.dot_general(v, kr, TN, preferred_element_type=F32)


def _gla_body(qf, kf, vf, gdf, qb, kb, vb, gdb, upf, bsf, upb, bsb, of_ref, ob_ref,
              stf, stb, laf, lab, *, nchunk):
    @pl.when(pl.program_id(2) == 0)
    def _():
        stf[...] = jnp.zeros_like(stf)
        stb[...] = jnp.zeros_like(stb)

    C = GLA_CHUNK
    row = lax.broadcasted_iota(I32, (C, C), 0)
    col = lax.broadcasted_iota(I32, (C, C), 1)
    lower = row >= col
    upper = col >= row
    _chunk_cumsum(_log_gate(gdf[:, 0:GLA_RANK], upf, bsf), lower, laf, nchunk)
    _chunk_cumsum(_log_gate(gdb[:, GLA_RANK:2 * GLA_RANK], upb, bsb), upper, lab, nchunk)

    def step(c, carry):
        _gla_chunk(qf, kf, vf, laf, stf, of_ref, c * C, lower, C - 1)
        _gla_chunk(qb, kb, vb, lab, stb, ob_ref, (nchunk - 1 - c) * C, upper, 0)
        return carry

    lax.fori_loop(0, nchunk, step, 0, unroll=16)


def _gla(main3, gd3, up_f, bias_f, up_b, bias_b, lb=1024):
    B, S, _ = main3.shape
    nb = S // lb
    kq = GLA_QK // GLA_DK
    kv = (2 * GLA_QK) // GLA_DV
    fwd = lambda b, h, i: (b, i, h)
    blk = lambda c0: (lambda b, h, i: (b, i, c0 + h))
    rblk = lambda c0: (lambda b, h, i: (b, nb - 1 - i, c0 + h))
    sq = lambda m: pl.BlockSpec((None, lb, GLA_DK), m)
    sv = lambda m: pl.BlockSpec((None, lb, GLA_DV), m)
    sg = lambda m: pl.BlockSpec((None, lb, GD_W), m)
    up_spec = pl.BlockSpec((GLA_RANK, GLA_DK), lambda b, h, i: (0, h))
    bias_spec = pl.BlockSpec((1, GLA_DK), lambda b, h, i: (0, h))
    out_sds = jax.ShapeDtypeStruct((B, S, GLA_V), F32)
    return pl.pallas_call(
        functools.partial(_gla_body, nchunk=lb // GLA_CHUNK),
        out_shape=(out_sds, out_sds),
        grid=(B, GLA_HEADS, nb),
        in_specs=[sq(blk(0)), sq(blk(kq)), sv(blk(kv)), sg(lambda b, h, i: (b, i, 0)),
                  sq(rblk(0)), sq(rblk(kq)), sv(rblk(kv)), sg(lambda b, h, i: (b, nb - 1 - i, 0)),
                  up_spec, bias_spec, up_spec, bias_spec],
        out_specs=(pl.BlockSpec((None, lb, GLA_DV), fwd),
                   pl.BlockSpec((None, lb, GLA_DV), lambda b, h, i: (b, nb - 1 - i, h))),
        scratch_shapes=[pltpu.VMEM((GLA_DV, GLA_DK), F32), pltpu.VMEM((GLA_DV, GLA_DK), F32),
                        pltpu.VMEM((lb, GLA_DK), F32), pltpu.VMEM((lb, GLA_DK), F32)],
        compiler_params=_cparams(3),
        name="gla",
    )(main3, main3, main3, gd3, main3, main3, main3, gd3, up_f, bias_f, up_b, bias_b)


ATT_TQ = 128
ATT_WIN = 256


ATT_GROUP = 32


ATT_FACTOR = 4
assert all(b == a * ATT_FACTOR for a, b in zip(ATT_DILATIONS[:-1], ATT_DILATIONS[1:])) and ATT_DILATIONS[0] == 1


def _split_classes(src, dst, d, n):
    f = ATT_FACTOR
    sub = n // (d * f)
    for r in range(d):
        for a in range(f):
            dst[pl.ds((d * a + r) * sub, sub), :] = src[pl.ds(r * (n // d) + a, sub, stride=f), :]


def _merge_classes(src, dst, d, n):
    f = ATT_FACTOR
    sub = n // (d * f)
    for r in range(d):
        for a in range(f):
            dst[pl.ds(r * (n // d) + a, sub, stride=f), :] = src[pl.ds((d * a + r) * sub, sub), :]


def _attn_body(slopes_ref, q_ref, k_ref, v_ref, o_ref, *scr, seq):
    pair = pl.program_id(1)
    lane = lax.broadcasted_iota(I32, (1, LANES), 1)
    head0 = lane < ATT_HD
    ii = lax.broadcasted_iota(I32, (ATT_TQ, ATT_WIN), 0)
    jj = lax.broadcasted_iota(I32, (ATT_TQ, ATT_WIN), 1)
    slope = (slopes_ref[2 * pair], slopes_ref[2 * pair + 1])
    npat = len(ATT_DILATIONS)
    copies = [(q_ref, k_ref, v_ref)] + [scr[3 * i:3 * i + 3] for i in range(npat - 1)]
    stats = [scr[3 * (npat - 1):3 * npat], scr[3 * npat:3 * (npat + 1)]]
    bias_s = scr[3 * (npat + 1)]
    for pi in range(1, npat):
        for x in range(3):
            _split_classes(copies[pi - 1][x], copies[pi][x], ATT_DILATIONS[pi - 1], seq)

    for step, pi in enumerate(reversed(range(npat))):
        d = ATT_DILATIONS[pi]
        L = seq // d
        nblk = L // ATT_TQ
        first = step == 0
        q_ref, k_ref, v_ref = copies[pi]
        if not first:
            for x in range(3):
                _merge_classes(stats[(step - 1) % 2][x], stats[step % 2][x], d, seq)
        m_s, l_s, acc_s = stats[step % 2]
        for case in range(3):
            ad = jnp.abs(jj - ii - case * ATT_RADIUS)
            for hh in range(2):
                bias_s[case, hh] = jnp.where(ad <= ATT_RADIUS, -slope[hh] * (ad * d).astype(F32), NEG)

        def group(gi, carry, L=L, nblk=nblk, first=first, q_ref=q_ref, k_ref=k_ref, v_ref=v_ref,
                  m_s=m_s, l_s=l_s, acc_s=acc_s):
            loaded = []
            for u in range(ATT_GROUP):
                t = gi * ATT_GROUP + u
                base = (t // nblk) * L
                q0 = (t % nblk) * ATT_TQ
                start = jnp.clip(q0 - ATT_RADIUS, 0, L - ATT_WIN)
                case = (q0 - start) // ATT_RADIUS
                qsl = pl.ds(pl.multiple_of(base + q0, ATT_TQ), ATT_TQ)
                ksl = pl.ds(pl.multiple_of(base + start, ATT_RADIUS), ATT_WIN)
                q = q_ref[qsl, :] * (ATT_HD ** -0.5)
                k = k_ref[ksl, :].astype(BF16)
                v = v_ref[ksl, :].astype(BF16)
                run = None if first else (m_s[qsl, :], l_s[qsl, :], acc_s[qsl, :])
                loaded.append((qsl, case, q, k, v, run))
            results = []
            for qsl, case, q, k, v, run in loaded:
                m_new, lsum, pv = [], [], []
                for hh in range(2):
                    hm = head0 if hh == 0 else jnp.logical_not(head0)
                    qh = jnp.where(hm, q, 0.0).astype(BF16)
                    s = lax.dot_general(qh, k, NT, preferred_element_type=F32) + bias_s[case, hh]
                    mh = jnp.max(s, axis=-1, keepdims=True)
                    p = jnp.exp(s - mh)
                    m_new.append(mh)
                    lsum.append(jnp.sum(p, axis=-1, keepdims=True))
                    pv.append(jnp.dot(p.astype(BF16), v, preferred_element_type=F32))
                m_full = jnp.where(head0, m_new[0], m_new[1])
                l_blk = jnp.where(head0, lsum[0], lsum[1])
                pv_blk = jnp.where(head0, pv[0], pv[1])
                if not first:
                    m_blk = m_full
                    m_full = jnp.maximum(run[0], m_blk)
                    a_run = jnp.exp(run[0] - m_full)
                    a_blk = jnp.exp(m_blk - m_full)
                    l_blk = a_run * run[1] + a_blk * l_blk
                    pv_blk = a_run * run[2] + a_blk * pv_blk
                results.append((qsl, m_full, l_blk, pv_blk))
            for qsl, m_full, l_blk, pv_blk in results:
                m_s[qsl, :] = m_full
                l_s[qsl, :] = l_blk
                acc_s[qsl, :] = pv_blk
            return carry

        lax.fori_loop(0, d * nblk // ATT_GROUP, group, 0)

    o_ref[...] = acc_s[...] / l_s[...]


def _attn(main3, slopes):
    B, S, _ = main3.shape
    c0 = (2 * GLA_QK + 2 * GLA_V) // LANES
    npair = ATT_HEADS // 2
    spec = lambda off: pl.BlockSpec((None, S, LANES), lambda b, p: (b, 0, c0 + off + p))
    return pl.pallas_call(
        functools.partial(_attn_body, seq=S),
        out_shape=jax.ShapeDtypeStruct((B, S, ATT_W), F32),
        grid=(B, npair),
        in_specs=[pl.BlockSpec(memory_space=pltpu.SMEM), spec(0), spec(npair), spec(2 * npair)],
        out_specs=pl.BlockSpec((None, S, LANES), lambda b, p: (b, 0, p)),
        scratch_shapes=([pltpu.VMEM((S, LANES), F32)] * (3 * (len(ATT_DILATIONS) + 1))
                        + [pltpu.VMEM((3, 2, ATT_TQ, ATT_WIN), F32)]),
        compiler_params=_cparams(2),
        name="attn",
    )(slopes, main3, main3, main3)


def _mix_body(of_ref, ob_ref, gr_ref, att_ref, x_ref, gg_ref, ag_ref, wo_ref, n2_ref, h_ref, hn_ref):
    o = of_ref[...] + ob_ref[...]
    parts = [_rms(o[:, h * GLA_DV:(h + 1) * GLA_DV], gg_ref[...]) for h in range(GLA_HEADS)]
    on = jnp.concatenate(parts, axis=-1)
    gr = gr_ref[...]
    gla = on * (gr / (1.0 + jnp.exp(-gr)))
    an = _rms(att_ref[...], ag_ref[...])
    y = (jnp.dot(gla.astype(BF16), wo_ref[0:GLA_V, :], preferred_element_type=F32)
         + jnp.dot(an.astype(BF16), wo_ref[GLA_V:GLA_V + ATT_W, :], preferred_element_type=F32))
    h = x_ref[...] + y
    h_ref[...] = h
    hn_ref[...] = _rms(h, n2_ref[...]).astype(BF16)


def _mix(o_f, o_b, main2, att2, x2, gg, ag, w_out, n2g, tm=512):
    T = x2.shape[0]
    gr_blk = (2 * GLA_QK + GLA_V) // GLA_V
    row = lambda i: (i, 0)
    const = lambda i: (0, 0)
    return pl.pallas_call(
        _mix_body,
        out_shape=(jax.ShapeDtypeStruct((T, D_MODEL), F32), jax.ShapeDtypeStruct((T, D_MODEL), BF16)),
        grid=(T // tm,),
        in_specs=[pl.BlockSpec((tm, GLA_V), row), pl.BlockSpec((tm, GLA_V), row),
                  pl.BlockSpec((tm, GLA_V), lambda i: (i, gr_blk)),
                  pl.BlockSpec((tm, ATT_W), row), pl.BlockSpec((tm, D_MODEL), row),
                  pl.BlockSpec((1, GLA_DV), const), pl.BlockSpec((1, ATT_W), const),
                  pl.BlockSpec((D_MODEL, D_MODEL), const), pl.BlockSpec((1, D_MODEL), const)],
        out_specs=(pl.BlockSpec((tm, D_MODEL), row), pl.BlockSpec((tm, D_MODEL), row)),
        compiler_params=_cparams(1),
        name="mix",
    )(o_f, o_b, main2, att2, x2, gg, ag, w_out, n2g)


ROUTE_TM = SUBLANES * LANES
_CELLS = [(a, b) for a in range(PEER_TOPK) for b in range(PEER_TOPK) if (a + 1) * (b + 1) <= PEER_TOPK]


def _key_rows(i):
    return pl.ds(i * SUBLANES, SUBLANES)


def _before(a, b):
    (va, ia), (vb, ib) = a, b
    if isinstance(ia, float) and isinstance(ib, float):
        return (va >= vb) if ia < ib else (va > vb)
    return jnp.logical_or(va > vb, jnp.logical_and(va == vb, ia < ib))


def _pick(c, a, b):
    return tuple(jnp.where(c, x, y) for x, y in zip(a, b))


def _ordered_pair(a, b):
    c = _before(a, b)
    return _pick(c, a, b), _pick(c, b, a)


def _bitonic_merge(xs):
    n = len(xs)
    if n == 1:
        return list(xs)
    h = n // 2
    xs = list(xs)
    for k in range(h):
        xs[k], xs[k + h] = _ordered_pair(xs[k], xs[k + h])
    return _bitonic_merge(xs[:h]) + _bitonic_merge(xs[h:])


def _bitonic_sort(xs):
    n = len(xs)
    if n == 1:
        return list(xs)
    return _bitonic_merge(_bitonic_sort(xs[:n // 2]) + _bitonic_sort(xs[n // 2:])[::-1])


def _top_keys(s_ref):
    k = PEER_TOPK
    best = None
    for g in range(PEER_NKEYS // k):
        grp = _bitonic_sort([(s_ref[_key_rows(g * k + a), :], float(g * k + a)) for a in range(k)])
        grp = [(v, jnp.full((SUBLANES, LANES), i, F32) if isinstance(i, float) else i) for v, i in grp]
        if best is None:
            best = grp
        else:
            best = _bitonic_merge([_pick(_before(best[j], grp[k - 1 - j]), best[j], grp[k - 1 - j])
                                   for j in range(k)])
    return [v for v, _ in best], [i for _, i in best]


def _to_token_lanes(src_ref, dst_ref):
    for g in range(SUBLANES):
        dst_ref[g] = src_ref[pl.ds(g, PEER_NKEYS, stride=SUBLANES), :].astype(dst_ref.dtype)


def _route_body(hn_ref, wq_ref, keys_ref, r2_ref, e2_ref, n1_ref, c1_ref, s1_s, s2_s, rk2_s, n1_s):
    q = jnp.dot(hn_ref[...], wq_ref[...], preferred_element_type=F32)
    for c, s_s in enumerate((s1_s, s2_s)):
        qc = q[:, c * PEER_HALF:(c + 1) * PEER_HALF].astype(BF16)
        st = lax.dot_general(keys_ref[c], qc, NT, preferred_element_type=F32)
        for g in range(SUBLANES):
            s_s[pl.ds(g, PEER_NKEYS, stride=SUBLANES), :] = st[:, g * LANES:(g + 1) * LANES]
    sc1, si1 = _top_keys(s1_s)
    sc2, si2 = _top_keys(s2_s)
    e1 = [jnp.exp(sc1[k] - sc1[0]) for k in range(PEER_TOPK)]
    e2 = [jnp.exp(sc2[k] - sc2[0]) for k in range(PEER_TOPK)]

    cand = [sc1[a] + sc2[b] for (a, b) in _CELLS]
    nc = len(_CELLS)
    before = [jnp.zeros((SUBLANES, LANES), F32) for _ in range(nc)]
    for x in range(nc):
        for y in range(x + 1, nc):
            ax, bx = _CELLS[x]
            ay, by = _CELLS[y]
            if ax <= ay and bx <= by:
                before[y] = before[y] + 1.0
                continue
            bt = jnp.where(cand[x] >= cand[y], 1.0, 0.0)
            before[y] = before[y] + bt
            before[x] = before[x] + (1.0 - bt)
    zero = jnp.zeros((SUBLANES, LANES), F32)
    ncol = [zero] * PEER_TOPK
    zsum = zero
    for x, (a, b) in enumerate(_CELLS):
        sel = jnp.where(before[x] < float(PEER_TOPK), 1.0, 0.0)
        ncol[a] = ncol[a] + sel
        zsum = zsum + sel * (e1[a] * e2[b])
    inv_z = 1.0 / zsum

    def dense(i, carry):
        rows = _key_rows(i)
        key = lax.convert_element_type(i, F32)
        n1 = zero
        rk2 = jnp.full((SUBLANES, LANES), float(PEER_TOPK), F32)
        for k in range(PEER_TOPK):
            n1 = jnp.where(si1[k] == key, ncol[k], n1)
            rk2 = jnp.where(si2[k] == key, float(k), rk2)
        n1_s[rows, :] = n1
        rk2_s[rows, :] = rk2
        s1_s[rows, :] = jnp.exp(s1_s[rows, :] - sc1[0]) * inv_z
        s2_s[rows, :] = jnp.exp(s2_s[rows, :] - sc2[0])
        return carry

    lax.fori_loop(0, PEER_NKEYS, dense, 0, unroll=4)
    _to_token_lanes(rk2_s, r2_ref)
    _to_token_lanes(s2_s, e2_ref)
    _to_token_lanes(n1_s, n1_ref)
    _to_token_lanes(s1_s, c1_ref)


def _route(hn, w_q, keys):
    T = hn.shape[0]
    tm = ROUTE_TM
    out_sds = lambda dt: jax.ShapeDtypeStruct((PEER_HEADS, T // LANES, PEER_NKEYS, LANES), dt)
    out_spec = pl.BlockSpec((None, SUBLANES, PEER_NKEYS, LANES), lambda i, h: (h, i, 0, 0))
    km = pltpu.VMEM((PEER_NKEYS * SUBLANES, LANES), F32)
    return pl.pallas_call(
        _route_body,
        out_shape=(out_sds(F32),) * 4,
        grid=(T // tm, PEER_HEADS),
        in_specs=[pl.BlockSpec((tm, D_MODEL), lambda i, h: (i, 0)),
                  pl.BlockSpec((D_MODEL, 2 * PEER_HALF), lambda i, h: (0, h)),
                  pl.BlockSpec((None, 2, PEER_NKEYS, PEER_HALF), lambda i, h: (h, 0, 0, 0))],
        out_specs=(out_spec,) * 4,
        scratch_shapes=[km] * 4,
        compiler_params=_cparams(2),
        name="route",
    )(hn, w_q, keys)


PEER_TE = 512
PEER_QROWS = 32
INV_SQRT2 = 0.7071067811865476


def _peer_body(hn_ref, u_ref, vt_ref, r2_ref, e2_ref, n1_ref, c1_ref, o_ref, a_scr, g_scr, r2_b, e2_b):
    j = pl.program_id(1)

    @pl.when(j == 0)
    def _():
        o_ref[...] = jnp.zeros_like(o_ref)
        r2_b[...] = r2_ref[...].astype(BF16)
        e2_b[...] = e2_ref[...].astype(BF16)

    n_il = PEER_TE // PEER_NKEYS
    packed = 2 * SUBLANES
    reps = PEER_QROWS // packed
    n_qd = PEER_NKEYS // PEER_QROWS
    a_scr[...] = lax.dot_general(u_ref[...], hn_ref[...], NT, preferred_element_type=F32)
    for il0 in range(0, n_il, 2):
        ils = (il0, il0 + 1)
        for cb in range(g_scr.shape[1] // LANES):
            lanes = slice(cb * LANES, (cb + 1) * LANES)
            w = {(il, qd): None for il in ils for qd in range(n_qd)}
            for h in range(PEER_HEADS):
                row = {}
                for il in ils:
                    key = pl.ds(j * n_il + il, packed, stride=0)
                    row[il] = (jnp.concatenate([n1_ref[h, cb, key, :].astype(BF16)] * reps, axis=0),
                               jnp.concatenate([c1_ref[h, cb, key, :].astype(BF16)] * reps, axis=0))
                for qd in range(n_qd):
                    rows = slice(qd * PEER_QROWS, (qd + 1) * PEER_QROWS)
                    r2 = r2_b[h, cb, rows, :]
                    e2 = e2_b[h, cb, rows, :]
                    for il in ils:
                        t = jnp.where(r2 < row[il][0], e2, jnp.zeros_like(e2)) * row[il][1]
                        w[il, qd] = t if w[il, qd] is None else w[il, qd] + t
            for il in ils:
                for qd in range(n_qd):
                    er = slice(il * PEER_NKEYS + qd * PEER_QROWS, il * PEER_NKEYS + (qd + 1) * PEER_QROWS)
                    a = a_scr[er, lanes]
                    act = 0.5 * a * (1.0 + lax.erf(a * INV_SQRT2))
                    g_scr[er, lanes] = act.astype(BF16) * w[il, qd]
    o_ref[...] += jnp.dot(vt_ref[...], g_scr[...], preferred_element_type=F32)


def _peer(hn, u, vt, r2, e2, n1, c1, tm=1024):
    T = hn.shape[0]
    te = PEER_TE
    nj = PEER_EXPERTS // te
    rspec = pl.BlockSpec((PEER_HEADS, tm // LANES, PEER_NKEYS, LANES), lambda i, j: (0, i, 0, 0),
                         pipeline_mode=pl.Buffered(1))
    return pl.pallas_call(
        _peer_body,
        out_shape=jax.ShapeDtypeStruct((D_MODEL, T), F32),
        grid=(T // tm, nj),
        in_specs=[pl.BlockSpec((tm, D_MODEL), lambda i, j: (i, 0)),
                  pl.BlockSpec((te, D_MODEL), lambda i, j: (j, 0)),
                  pl.BlockSpec((None, D_MODEL, te), lambda i, j: (j, 0, 0)),
                  rspec, rspec, rspec, rspec],
        out_specs=pl.BlockSpec((D_MODEL, tm), lambda i, j: (0, i)),
        scratch_shapes=([pltpu.VMEM((te, tm), F32), pltpu.VMEM((te, tm), BF16)]
                        + [pltpu.VMEM((PEER_HEADS, tm // LANES, PEER_NKEYS, LANES), BF16)] * 2),
        compiler_params=_cparams(2),
        name="peer",
    )(hn, u, vt, r2, e2, n1, c1)


def _final_body(h_ref, pt_ref, g_ref, o_ref):
    o_ref[...] = _rms(h_ref[...] + pt_ref[...].T, g_ref[...])


def _final(h, peer_t, g, tm=512):
    T = h.shape[0]
    return pl.pallas_call(
        _final_body,
        out_shape=jax.ShapeDtypeStruct((T, D_MODEL), F32),
        grid=(T // tm,),
        in_specs=[pl.BlockSpec((tm, D_MODEL), lambda i: (i, 0)),
                  pl.BlockSpec((D_MODEL, tm), lambda i: (0, i)),
                  pl.BlockSpec((1, D_MODEL), lambda i: (0, 0))],
        out_specs=pl.BlockSpec((tm, D_MODEL), lambda i: (i, 0)),
        compiler_params=_cparams(1),
        name="final",
    )(h, peer_t, g)


def _layer(x, norm1_g, w_in, up_f, bias_f, up_b, bias_b, gla_norm_g, att_norm_g, w_out, norm2_g,
           w_q, sub_keys, peer_u, peer_v):
    B, S, D = x.shape
    T = B * S
    x2 = x.reshape(T, D)
    gd0 = 2 * GLA_QK + 2 * GLA_V
    w_main = jnp.concatenate([w_in[:, :gd0], w_in[:, gd0 + GD_W:]], axis=1).astype(BF16)
    w_gd = w_in[:, gd0:gd0 + GD_W].astype(BF16)
    main, gd = _inproj(x2, norm1_g.reshape(1, D), w_main, w_gd)
    main3 = main.reshape(B, S, MAIN_W)
    o_f, o_b = _gla(main3, gd.reshape(B, S, GD_W), up_f, bias_f.reshape(1, GLA_QK),
                    up_b, bias_b.reshape(1, GLA_QK))
    slopes = jnp.asarray((2.0 ** (-8.0 * np.arange(1, ATT_HEADS + 1) / ATT_HEADS)).astype(np.float32))
    att = _attn(main3, slopes)
    h, hn = _mix(o_f.reshape(T, GLA_V), o_b.reshape(T, GLA_V), main, att.reshape(T, ATT_W), x2,
                 gla_norm_g.reshape(1, GLA_DV), att_norm_g.reshape(1, ATT_W), w_out.astype(BF16),
                 norm2_g.reshape(1, D))
    r2, e2, n1, c1 = _route(hn, w_q.astype(BF16), sub_keys.astype(BF16))
    vt = peer_v.astype(BF16).reshape(PEER_EXPERTS // PEER_TE, PEER_TE, D).transpose(0, 2, 1)
    peer_t = _peer(hn, peer_u.astype(BF16), vt, r2, e2, n1, c1)
    return h, peer_t


def kernel(x, norm1_g, w_in, gla_gate_up_f, gla_gate_bias_f, gla_gate_up_b, gla_gate_bias_b, gla_norm_g,
           att_norm_g, w_out, norm2_g, peer_w_q, peer_sub_keys, peer_u, peer_v, final_norm_g):
    B, S, D = x.shape
    assert norm1_g.shape[0] == 1, "single trunk layer: the final norm is fused with the layer's last residual"
    h, peer_t = _layer(x, norm1_g[0], w_in[0], gla_gate_up_f[0], gla_gate_bias_f[0], gla_gate_up_b[0],
                       gla_gate_bias_b[0], gla_norm_g[0], att_norm_g[0], w_out[0], norm2_g[0],
                       peer_w_q[0], peer_sub_keys[0], peer_u[0], peer_v[0])
    return _final(h, peer_t, final_norm_g.reshape(1, D)).reshape(B, S, D)
```

```python
import functools

import numpy as np
import jax
import jax.numpy as jnp
from jax import lax
from jax.experimental import pallas as pl
from jax.experimental.pallas import tpu as pltpu

F32 = jnp.float32
BF16 = jnp.bfloat16
I32 = jnp.int32

D_MODEL = 2048
GLA_HEADS = 4
GLA_DK = 128
GLA_DV = 256
GLA_RANK = 16
GLA_TAU = 16.0
GLA_CHUNK = 64
GLA_QK = GLA_HEADS * GLA_DK
GLA_V = GLA_HEADS * GLA_DV
ATT_HEADS = 16
ATT_HD = 64
ATT_W = ATT_HEADS * ATT_HD
ATT_DILATIONS = (1, 4, 16)
ATT_RADIUS = 64
PEER_NKEYS = 128
PEER_HEADS = 8
PEER_TOPK = 16
PEER_EXPERTS = PEER_NKEYS * PEER_NKEYS
PEER_HALF = 128
EPS = 1e-6
NEG = -1e30
MAIN_W = 2 * GLA_QK + 2 * GLA_V + 3 * ATT_W
GD_W = 2 * GLA_RANK

LANES = 128
SUBLANES = 8
VMEM_LIMIT = 56 * 1024 * 1024

NT = (((1,), (1,)), ((), ()))
TN = (((0,), (0,)), ((), ()))


def _cparams(n_axes):
    return pltpu.CompilerParams(dimension_semantics=("arbitrary",) * n_axes, vmem_limit_bytes=VMEM_LIMIT)


def _rms(x, g):
    ms = jnp.mean(x * x, axis=-1, keepdims=True)
    return x * lax.rsqrt(ms + EPS) * g


def _inproj_body(x_ref, g_ref, w_ref, wgd_ref, o_ref, gd_ref, xn_ref):
    @pl.when(pl.program_id(1) == 0)
    def _():
        xn = _rms(x_ref[...], g_ref[...]).astype(BF16)
        xn_ref[...] = xn
        gd_ref[...] = jnp.dot(xn, wgd_ref[...], preferred_element_type=F32)

    o_ref[...] = jnp.dot(xn_ref[...], w_ref[...], preferred_element_type=F32)


def _inproj(x2, g, w_main, w_gd, tm=1024, tn=1536):
    T = x2.shape[0]
    return pl.pallas_call(
        _inproj_body,
        out_shape=(jax.ShapeDtypeStruct((T, MAIN_W), F32), jax.ShapeDtypeStruct((T, GD_W), F32)),
        grid=(T // tm, MAIN_W // tn),
        in_specs=[pl.BlockSpec((tm, D_MODEL), lambda i, j: (i, 0)),
                  pl.BlockSpec((1, D_MODEL), lambda i, j: (0, 0)),
                  pl.BlockSpec((D_MODEL, tn), lambda i, j: (0, j)),
                  pl.BlockSpec((D_MODEL, GD_W), lambda i, j: (0, 0))],
        out_specs=(pl.BlockSpec((tm, tn), lambda i, j: (i, j)),
                   pl.BlockSpec((tm, GD_W), lambda i, j: (i, 0))),
        scratch_shapes=[pltpu.VMEM((tm, D_MODEL), BF16)],
        compiler_params=_cparams(2),
        name="inproj",
    )(x2, g, w_main, w_gd)


def _log_gate(gd, up_ref, bias_ref):
    z = jnp.dot(gd.astype(BF16), up_ref[...].astype(BF16), preferred_element_type=F32) + bias_ref[...]
    return (jnp.minimum(z, 0.0) - jnp.log1p(jnp.exp(-jnp.abs(z)))) * (1.0 / GLA_TAU)


def _chunk_cumsum(la, tri_mask, b_ref, nchunk):
    C = GLA_CHUNK
    hi = la.astype(BF16)
    lo = (la - hi.astype(F32)).astype(BF16)
    cols = []
    for c in range(nchunk):
        cols += [hi[c * C:(c + 1) * C, :], lo[c * C:(c + 1) * C, :]]
    tri = jnp.where(tri_mask, 1.0, 0.0).astype(BF16)
    both = jnp.dot(tri, jnp.concatenate(cols, axis=1), preferred_element_type=F32)
    for c in range(nchunk):
        b_ref[c * C:(c + 1) * C, :] = (both[:, 2 * c * GLA_DK:(2 * c + 1) * GLA_DK]
                                       + both[:, (2 * c + 1) * GLA_DK:(2 * c + 2) * GLA_DK])


def _gla_chunk(q_ref, k_ref, v_ref, b_ref, st_ref, o_ref, r0, tri_mask, tot_row):
    C = GLA_CHUNK
    sl = pl.ds(pl.multiple_of(r0, C), C)
    b = b_ref[sl, :]
    btot = b[tot_row:tot_row + 1, :]
    q = q_ref[sl, :]
    k = k_ref[sl, :]
    v = v_ref[sl, :].astype(BF16)
    qd = (q * (jnp.exp(b) * (GLA_DK ** -0.5))).astype(BF16)
    kd = (k * jnp.exp(-b)).astype(BF16)
    kr = (k * jnp.exp(btot - b)).astype(BF16)
    att = lax.dot_general(qd, kd, NT, preferred_element_type=F32)
    att = jnp.where(tri_mask, att, 0.0).astype(BF16)
    st = st_ref[...]
    o = (jnp.dot(att, v, preferred_element_type=F32)
         + lax.dot_general(qd, st.astype(BF16), NT, preferred_element_type=F32))
    o_ref[sl, :] = o
    st_ref[...] = jnp.exp(btot) * st + lax.dot_general(v, kr, TN, preferred_element_type=F32)


def _gla_body(qf, kf, vf, gdf, qb, kb, vb, gdb, upf, bsf, upb, bsb, of_ref, ob_ref,
              stf, stb, laf, lab, *, nchunk):
    @pl.when(pl.program_id(2) == 0)
    def _():
        stf[...] = jnp.zeros_like(stf)
        stb[...] = jnp.zeros_like(stb)

    C = GLA_CHUNK
    row = lax.broadcasted_iota(I32, (C, C), 0)
    col = lax.broadcasted_iota(I32, (C, C), 1)
    lower = row >= col
    upper = col >= row
    _chunk_cumsum(_log_gate(gdf[:, 0:GLA_RANK], upf, bsf), lower, laf, nchunk)
    _chunk_cumsum(_log_gate(gdb[:, GLA_RANK:2 * GLA_RANK], upb, bsb), upper, lab, nchunk)

    def step(c, carry):
        _gla_chunk(qf, kf, vf, laf, stf, of_ref, c * C, lower, C - 1)
        _gla_chunk(qb, kb, vb, lab, stb, ob_ref, (nchunk - 1 - c) * C, upper, 0)
        return carry

    lax.fori_loop(0, nchunk, step, 0, unroll=16)


def _gla(main3, gd3, up_f, bias_f, up_b, bias_b, lb=1024):
    B, S, _ = main3.shape
    nb = S // lb
    kq = GLA_QK // GLA_DK
    kv = (2 * GLA_QK) // GLA_DV
    fwd = lambda b, h, i: (b, i, h)
    blk = lambda c0: (lambda b, h, i: (b, i, c0 + h))
    rblk = lambda c0: (lambda b, h, i: (b, nb - 1 - i, c0 + h))
    sq = lambda m: pl.BlockSpec((None, lb, GLA_DK), m)
    sv = lambda m: pl.BlockSpec((None, lb, GLA_DV), m)
    sg = lambda m: pl.BlockSpec((None, lb, GD_W), m)
    up_spec = pl.BlockSpec((GLA_RANK, GLA_DK), lambda b, h, i: (0, h))
    bias_spec = pl.BlockSpec((1, GLA_DK), lambda b, h, i: (0, h))
    out_sds = jax.ShapeDtypeStruct((B, S, GLA_V), F32)
    return pl.pallas_call(
        functools.partial(_gla_body, nchunk=lb // GLA_CHUNK),
        out_shape=(out_sds, out_sds),
        grid=(B, GLA_HEADS, nb),
        in_specs=[sq(blk(0)), sq(blk(kq)), sv(blk(kv)), sg(lambda b, h, i: (b, i, 0)),
                  sq(rblk(0)), sq(rblk(kq)), sv(rblk(kv)), sg(lambda b, h, i: (b, nb - 1 - i, 0)),
                  up_spec, bias_spec, up_spec, bias_spec],
        out_specs=(pl.BlockSpec((None, lb, GLA_DV), fwd),
                   pl.BlockSpec((None, lb, GLA_DV), lambda b, h, i: (b, nb - 1 - i, h))),
        scratch_shapes=[pltpu.VMEM((GLA_DV, GLA_DK), F32), pltpu.VMEM((GLA_DV, GLA_DK), F32),
                        pltpu.VMEM((lb, GLA_DK), F32), pltpu.VMEM((lb, GLA_DK), F32)],
        compiler_params=_cparams(3),
        name="gla",
    )(main3, main3, main3, gd3, main3, main3, main3, gd3, up_f, bias_f, up_b, bias_b)


ATT_TQ = 128
ATT_WIN = 256


ATT_GROUP = 32


ATT_FACTOR = 4
assert all(b == a * ATT_FACTOR for a, b in zip(ATT_DILATIONS[:-1], ATT_DILATIONS[1:])) and ATT_DILATIONS[0] == 1


def _split_classes(src, dst, d, n):
    f = ATT_FACTOR
    sub = n // (d * f)
    for r in range(d):
        for a in range(f):
            dst[pl.ds((d * a + r) * sub, sub), :] = src[pl.ds(r * (n // d) + a, sub, stride=f), :]


def _merge_classes(src, dst, d, n):
    f = ATT_FACTOR
    sub = n // (d * f)
    for r in range(d):
        for a in range(f):
            dst[pl.ds(r * (n // d) + a, sub, stride=f), :] = src[pl.ds((d * a + r) * sub, sub), :]


def _attn_body(slopes_ref, q_ref, k_ref, v_ref, o_ref, *scr, seq):
    pair = pl.program_id(1)
    lane = lax.broadcasted_iota(I32, (1, LANES), 1)
    head0 = lane < ATT_HD
    ii = lax.broadcasted_iota(I32, (ATT_TQ, ATT_WIN), 0)
    jj = lax.broadcasted_iota(I32, (ATT_TQ, ATT_WIN), 1)
    slope = (slopes_ref[2 * pair], slopes_ref[2 * pair + 1])
    npat = len(ATT_DILATIONS)
    copies = [(q_ref, k_ref, v_ref)] + [scr[3 * i:3 * i + 3] for i in range(npat - 1)]
    stats = [scr[3 * (npat - 1):3 * npat], scr[3 * npat:3 * (npat + 1)]]
    bias_s = scr[3 * (npat + 1)]
    for pi in range(1, npat):
        for x in range(3):
            _split_classes(copies[pi - 1][x], copies[pi][x], ATT_DILATIONS[pi - 1], seq)

    for step, pi in enumerate(reversed(range(npat))):
        d = ATT_DILATIONS[pi]
        L = seq // d
        nblk = L // ATT_TQ
        first = step == 0
        q_ref, k_ref, v_ref = copies[pi]
        if not first:
            for x in range(3):
                _merge_classes(stats[(step - 1) % 2][x], stats[step % 2][x], d, seq)
        m_s, l_s, acc_s = stats[step % 2]
        for case in range(3):
            ad = jnp.abs(jj - ii - case * ATT_RADIUS)
            for hh in range(2):
                bias_s[case, hh] = jnp.where(ad <= ATT_RADIUS, -slope[hh] * (ad * d).astype(F32), NEG)

        def group(gi, carry, L=L, nblk=nblk, first=first, q_ref=q_ref, k_ref=k_ref, v_ref=v_ref,
                  m_s=m_s, l_s=l_s, acc_s=acc_s):
            loaded = []
            for u in range(ATT_GROUP):
                t = gi * ATT_GROUP + u
                base = (t // nblk) * L
                q0 = (t % nblk) * ATT_TQ
                start = jnp.clip(q0 - ATT_RADIUS, 0, L - ATT_WIN)
                case = (q0 - start) // ATT_RADIUS
                qsl = pl.ds(pl.multiple_of(base + q0, ATT_TQ), ATT_TQ)
                ksl = pl.ds(pl.multiple_of(base + start, ATT_RADIUS), ATT_WIN)
                q = q_ref[qsl, :] * (ATT_HD ** -0.5)
                k = k_ref[ksl, :].astype(BF16)
                v = v_ref[ksl, :].astype(BF16)
                run = None if first else (m_s[qsl, :], l_s[qsl, :], acc_s[qsl, :])
                loaded.append((qsl, case, q, k, v, run))
            results = []
            for qsl, case, q, k, v, run in loaded:
                m_new, lsum, pv = [], [], []
                for hh in range(2):
                    hm = head0 if hh == 0 else jnp.logical_not(head0)
                    qh = jnp.where(hm, q, 0.0).astype(BF16)
                    s = lax.dot_general(qh, k, NT, preferred_element_type=F32) + bias_s[case, hh]
                    mh = jnp.max(s, axis=-1, keepdims=True)
                    p = jnp.exp(s - mh)
                    m_new.append(mh)
                    lsum.append(jnp.sum(p, axis=-1, keepdims=True))
                    pv.append(jnp.dot(p.astype(BF16), v, preferred_element_type=F32))
                m_full = jnp.where(head0, m_new[0], m_new[1])
                l_blk = jnp.where(head0, lsum[0], lsum[1])
                pv_blk = jnp.where(head0, pv[0], pv[1])
                if not first:
                    m_blk = m_full
                    m_full = jnp.maximum(run[0], m_blk)
                    a_run = jnp.exp(run[0] - m_full)
                    a_blk = jnp.exp(m_blk - m_full)
                    l_blk = a_run * run[1] + a_blk * l_blk
                    pv_blk = a_run * run[2] + a_blk * pv_blk
                results.append((qsl, m_full, l_blk, pv_blk))
            for qsl, m_full, l_blk, pv_blk in results:
                m_s[qsl, :] = m_full
                l_s[qsl, :] = l_blk
                acc_s[qsl, :] = pv_blk
            return carry

        lax.fori_loop(0, d * nblk // ATT_GROUP, group, 0)

    o_ref[...] = acc_s[...] / l_s[...]


def _attn(main3, slopes):
    B, S, _ = main3.shape
    c0 = (2 * GLA_QK + 2 * GLA_V) // LANES
    npair = ATT_HEADS // 2
    spec = lambda off: pl.BlockSpec((None, S, LANES), lambda b, p: (b, 0, c0 + off + p))
    return pl.pallas_call(
        functools.partial(_attn_body, seq=S),
        out_shape=jax.ShapeDtypeStruct((B, S, ATT_W), F32),
        grid=(B, npair),
        in_specs=[pl.BlockSpec(memory_space=pltpu.SMEM), spec(0), spec(npair), spec(2 * npair)],
        out_specs=pl.BlockSpec((None, S, LANES), lambda b, p: (b, 0, p)),
        scratch_shapes=([pltpu.VMEM((S, LANES), F32)] * (3 * (len(ATT_DILATIONS) + 1))
                        + [pltpu.VMEM((3, 2, ATT_TQ, ATT_WIN), F32)]),
        compiler_params=_cparams(2),
        name="attn",
    )(slopes, main3, main3, main3)


def _mix_body(of_ref, ob_ref, gr_ref, att_ref, x_ref, gg_ref, ag_ref, wo_ref, n2_ref, h_ref, hn_ref):
    o = of_ref[...] + ob_ref[...]
    parts = [_rms(o[:, h * GLA_DV:(h + 1) * GLA_DV], gg_ref[...]) for h in range(GLA_HEADS)]
    on = jnp.concatenate(parts, axis=-1)
    gr = gr_ref[...]
    gla = on * (gr / (1.0 + jnp.exp(-gr)))
    an = _rms(att_ref[...], ag_ref[...])
    y = (jnp.dot(gla.astype(BF16), wo_ref[0:GLA_V, :], preferred_element_type=F32)
         + jnp.dot(an.astype(BF16), wo_ref[GLA_V:GLA_V + ATT_W, :], preferred_element_type=F32))
    h = x_ref[...] + y
    h_ref[...] = h
    hn_ref[...] = _rms(h, n2_ref[...]).astype(BF16)


def _mix(o_f, o_b, main2, att2, x2, gg, ag, w_out, n2g, tm=512):
    T = x2.shape[0]
    gr_blk = (2 * GLA_QK + GLA_V) // GLA_V
    row = lambda i: (i, 0)
    const = lambda i: (0, 0)
    return pl.pallas_call(
        _mix_body,
        out_shape=(jax.ShapeDtypeStruct((T, D_MODEL), F32), jax.ShapeDtypeStruct((T, D_MODEL), BF16)),
        grid=(T // tm,),
        in_specs=[pl.BlockSpec((tm, GLA_V), row), pl.BlockSpec((tm, GLA_V), row),
                  pl.BlockSpec((tm, GLA_V), lambda i: (i, gr_blk)),
                  pl.BlockSpec((tm, ATT_W), row), pl.BlockSpec((tm, D_MODEL), row),
                  pl.BlockSpec((1, GLA_DV), const), pl.BlockSpec((1, ATT_W), const),
                  pl.BlockSpec((D_MODEL, D_MODEL), const), pl.BlockSpec((1, D_MODEL), const)],
        out_specs=(pl.BlockSpec((tm, D_MODEL), row), pl.BlockSpec((tm, D_MODEL), row)),
        compiler_params=_cparams(1),
        name="mix",
    )(o_f, o_b, main2, att2, x2, gg, ag, w_out, n2g)


ROUTE_TM = SUBLANES * LANES
_CELLS = [(a, b) for a in range(PEER_TOPK) for b in range(PEER_TOPK) if (a + 1) * (b + 1) <= PEER_TOPK]


def _key_rows(i):
    return pl.ds(i * SUBLANES, SUBLANES)


def _before(a, b):
    (va, ia), (vb, ib) = a, b
    if isinstance(ia, float) and isinstance(ib, float):
        return (va >= vb) if ia < ib else (va > vb)
    return jnp.logical_or(va > vb, jnp.logical_and(va == vb, ia < ib))


def _pick(c, a, b):
    return tuple(jnp.where(c, x, y) for x, y in zip(a, b))


def _ordered_pair(a, b):
    c = _before(a, b)
    return _pick(c, a, b), _pick(c, b, a)


def _bitonic_merge(xs):
    n = len(xs)
    if n == 1:
        return list(xs)
    h = n // 2
    xs = list(xs)
    for k in range(h):
        xs[k], xs[k + h] = _ordered_pair(xs[k], xs[k + h])
    return _bitonic_merge(xs[:h]) + _bitonic_merge(xs[h:])


def _bitonic_sort(xs):
    n = len(xs)
    if n == 1:
        return list(xs)
    return _bitonic_merge(_bitonic_sort(xs[:n // 2]) + _bitonic_sort(xs[n // 2:])[::-1])


def _top_keys(s_ref):
    k = PEER_TOPK
    best = None
    for g in range(PEER_NKEYS // k):
        grp = _bitonic_sort([(s_ref[_key_rows(g * k + a), :], float(g * k + a)) for a in range(k)])
        grp = [(v, jnp.full((SUBLANES, LANES), i, F32) if isinstance(i, float) else i) for v, i in grp]
        if best is None:
            best = grp
        else:
            best = _bitonic_merge([_pick(_before(best[j], grp[k - 1 - j]), best[j], grp[k - 1 - j])
                                   for j in range(k)])
    return [v for v, _ in best], [i for _, i in best]


def _to_token_lanes(src_ref, dst_ref):
    for g in range(SUBLANES):
        dst_ref[g] = src_ref[pl.ds(g, PEER_NKEYS, stride=SUBLANES), :].astype(dst_ref.dtype)


def _route_body(hn_ref, wq_ref, keys_ref, r2_ref, e2_ref, n1_ref, c1_ref, s1_s, s2_s, rk2_s, n1_s):
    q = jnp.dot(hn_ref[...], wq_ref[...], preferred_element_type=F32)
    for c, s_s in enumerate((s1_s, s2_s)):
        qc = q[:, c * PEER_HALF:(c + 1) * PEER_HALF].astype(BF16)
        st = lax.dot_general(keys_ref[c], qc, NT, preferred_element_type=F32)
        for g in range(SUBLANES):
            s_s[pl.ds(g, PEER_NKEYS, stride=SUBLANES), :] = st[:, g * LANES:(g + 1) * LANES]
    sc1, si1 = _top_keys(s1_s)
    sc2, si2 = _top_keys(s2_s)
    e1 = [jnp.exp(sc1[k] - sc1[0]) for k in range(PEER_TOPK)]
    e2 = [jnp.exp(sc2[k] - sc2[0]) for k in range(PEER_TOPK)]

    cand = [sc1[a] + sc2[b] for (a, b) in _CELLS]
    nc = len(_CELLS)
    before = [jnp.zeros((SUBLANES, LANES), F32) for _ in range(nc)]
    for x in range(nc):
        for y in range(x + 1, nc):
            ax, bx = _CELLS[x]
            ay, by = _CELLS[y]
            if ax <= ay and bx <= by:
                before[y] = before[y] + 1.0
                continue
            bt = jnp.where(cand[x] >= cand[y], 1.0, 0.0)
            before[y] = before[y] + bt
            before[x] = before[x] + (1.0 - bt)
    zero = jnp.zeros((SUBLANES, LANES), F32)
    ncol = [zero] * PEER_TOPK
    zsum = zero
    for x, (a, b) in enumerate(_CELLS):
        sel = jnp.where(before[x] < float(PEER_TOPK), 1.0, 0.0)
        ncol[a] = ncol[a] + sel
        zsum = zsum + sel * (e1[a] * e2[b])
    inv_z = 1.0 / zsum

    def dense(i, carry):
        rows = _key_rows(i)
        key = lax.convert_element_type(i, F32)
        n1 = zero
        rk2 = jnp.full((SUBLANES, LANES), float(PEER_TOPK), F32)
        for k in range(PEER_TOPK):
            n1 = jnp.where(si1[k] == key, ncol[k], n1)
            rk2 = jnp.where(si2[k] == key, float(k), rk2)
        n1_s[rows, :] = n1
        rk2_s[rows, :] = rk2
        s1_s[rows, :] = jnp.exp(s1_s[rows, :] - sc1[0]) * inv_z
        s2_s[rows, :] = jnp.exp(s2_s[rows, :] - sc2[0])
        return carry

    lax.fori_loop(0, PEER_NKEYS, dense, 0, unroll=4)
    _to_token_lanes(rk2_s, r2_ref)
    _to_token_lanes(s2_s, e2_ref)
    _to_token_lanes(n1_s, n1_ref)
    _to_token_lanes(s1_s, c1_ref)


def _route(hn, w_q, keys):
    T = hn.shape[0]
    tm = ROUTE_TM
    out_sds = lambda dt: jax.ShapeDtypeStruct((PEER_HEADS, T // LANES, PEER_NKEYS, LANES), dt)
    out_spec = pl.BlockSpec((None, SUBLANES, PEER_NKEYS, LANES), lambda i, h: (h, i, 0, 0))
    km = pltpu.VMEM((PEER_NKEYS * SUBLANES, LANES), F32)
    return pl.pallas_call(
        _route_body,
        out_shape=(out_sds(F32),) * 4,
        grid=(T // tm, PEER_HEADS),
        in_specs=[pl.BlockSpec((tm, D_MODEL), lambda i, h: (i, 0)),
                  pl.BlockSpec((D_MODEL, 2 * PEER_HALF), lambda i, h: (0, h)),
                  pl.BlockSpec((None, 2, PEER_NKEYS, PEER_HALF), lambda i, h: (h, 0, 0, 0))],
        out_specs=(out_spec,) * 4,
        scratch_shapes=[km] * 4,
        compiler_params=_cparams(2),
        name="route",
    )(hn, w_q, keys)


PEER_TE = 512
PEER_QROWS = 32
INV_SQRT2 = 0.7071067811865476


def _peer_body(hn_ref, u_ref, vt_ref, r2_ref, e2_ref, n1_ref, c1_ref, h_ref, fg_ref, out_ref,
               o_ref, a_scr, g_scr, r2_b, e2_b):
    j = pl.program_id(1)

    @pl.when(j == 0)
    def _():
        o_ref[...] = jnp.zeros_like(o_ref)
        r2_b[...] = r2_ref[...].astype(BF16)
        e2_b[...] = e2_ref[...].astype(BF16)

    n_il = PEER_TE // PEER_NKEYS
    packed = 2 * SUBLANES
    reps = PEER_QROWS // packed
    n_qd = PEER_NKEYS // PEER_QROWS
    a_scr[...] = lax.dot_general(u_ref[...], hn_ref[...], NT, preferred_element_type=F32)
    for il0 in range(0, n_il, 2):
        ils = (il0, il0 + 1)
        for cb in range(g_scr.shape[1] // LANES):
            lanes = slice(cb * LANES, (cb + 1) * LANES)
            w = {(il, qd): None for il in ils for qd in range(n_qd)}
            for h in range(PEER_HEADS):
                row = {}
                for il in ils:
                    key = pl.ds(j * n_il + il, packed, stride=0)
                    row[il] = (jnp.concatenate([n1_ref[h, cb, key, :].astype(BF16)] * reps, axis=0),
                               jnp.concatenate([c1_ref[h, cb, key, :].astype(BF16)] * reps, axis=0))
                for qd in range(n_qd):
                    rows = slice(qd * PEER_QROWS, (qd + 1) * PEER_QROWS)
                    r2 = r2_b[h, cb, rows, :]
                    e2 = e2_b[h, cb, rows, :]
                    for il in ils:
                        t = jnp.where(r2 < row[il][0], e2, jnp.zeros_like(e2)) * row[il][1]
                        w[il, qd] = t if w[il, qd] is None else w[il, qd] + t
            for il in ils:
                for qd in range(n_qd):
                    er = slice(il * PEER_NKEYS + qd * PEER_QROWS, il * PEER_NKEYS + (qd + 1) * PEER_QROWS)
                    a = a_scr[er, lanes]
                    act = 0.5 * a * (1.0 + lax.erf(a * INV_SQRT2))
                    g_scr[er, lanes] = act.astype(BF16) * w[il, qd]
    o_ref[...] += lax.dot_general(g_scr[...], vt_ref[...], TN, preferred_element_type=F32)

    @pl.when(j == pl.num_programs(1) - 1)
    def _():
        out_ref[...] = _rms(h_ref[...] + o_ref[...], fg_ref[...])


def _peer(hn, u, vt, r2, e2, n1, c1, h, fg, tm=512):
    T = hn.shape[0]
    te = PEER_TE
    nj = PEER_EXPERTS // te
    once = pl.Buffered(1)
    rspec = pl.BlockSpec((PEER_HEADS, tm // LANES, PEER_NKEYS, LANES), lambda i, j: (0, i, 0, 0), pipeline_mode=once)
    return pl.pallas_call(
        _peer_body,
        out_shape=jax.ShapeDtypeStruct((T, D_MODEL), F32),
        grid=(T // tm, nj),
        in_specs=[pl.BlockSpec((tm, D_MODEL), lambda i, j: (i, 0)),
                  pl.BlockSpec((te, D_MODEL), lambda i, j: (j, 0)),
                  pl.BlockSpec((te, D_MODEL), lambda i, j: (j, 0)),
                  rspec, rspec, rspec, rspec,
                  pl.BlockSpec((tm, D_MODEL), lambda i, j: (i, 0), pipeline_mode=once),
                  pl.BlockSpec((1, D_MODEL), lambda i, j: (0, 0))],
        out_specs=pl.BlockSpec((tm, D_MODEL), lambda i, j: (i, 0)),
        scratch_shapes=([pltpu.VMEM((tm, D_MODEL), F32), pltpu.VMEM((te, tm), F32), pltpu.VMEM((te, tm), BF16)]
                        + [pltpu.VMEM((PEER_HEADS, tm // LANES, PEER_NKEYS, LANES), BF16)] * 2),
        compiler_params=_cparams(2),
        name="peer",
    )(hn, u, vt, r2, e2, n1, c1, h, fg)


def _final_body(h_ref, pt_ref, g_ref, o_ref):
    o_ref[...] = _rms(h_ref[...] + pt_ref[...].T, g_ref[...])


def _final(h, peer_t, g, tm=512):
    T = h.shape[0]
    return pl.pallas_call(
        _final_body,
        out_shape=jax.ShapeDtypeStruct((T, D_MODEL), F32),
        grid=(T // tm,),
        in_specs=[pl.BlockSpec((tm, D_MODEL), lambda i: (i, 0)),
                  pl.BlockSpec((D_MODEL, tm), lambda i: (0, i)),
                  pl.BlockSpec((1, D_MODEL), lambda i: (0, 0))],
        out_specs=pl.BlockSpec((tm, D_MODEL), lambda i: (i, 0)),
        compiler_params=_cparams(1),
        name="final",
    )(h, peer_t, g)


def _layer(x, norm1_g, w_in, up_f, bias_f, up_b, bias_b, gla_norm_g, att_norm_g, w_out, norm2_g,
           w_q, sub_keys, peer_u, peer_v, final_norm_g):
    B, S, D = x.shape
    T = B * S
    x2 = x.reshape(T, D)
    gd0 = 2 * GLA_QK + 2 * GLA_V
    w_main = jnp.concatenate([w_in[:, :gd0], w_in[:, gd0 + GD_W:]], axis=1).astype(BF16)
    w_gd = w_in[:, gd0:gd0 + GD_W].astype(BF16)
    main, gd = _inproj(x2, norm1_g.reshape(1, D), w_main, w_gd)
    main3 = main.reshape(B, S, MAIN_W)
    o_f, o_b = _gla(main3, gd.reshape(B, S, GD_W), up_f, bias_f.reshape(1, GLA_QK),
                    up_b, bias_b.reshape(1, GLA_QK))
    slopes = jnp.asarray((2.0 ** (-8.0 * np.arange(1, ATT_HEADS + 1) / ATT_HEADS)).astype(np.float32))
    att = _attn(main3, slopes)
    h, hn = _mix(o_f.reshape(T, GLA_V), o_b.reshape(T, GLA_V), main, att.reshape(T, ATT_W), x2,
                 gla_norm_g.reshape(1, GLA_DV), att_norm_g.reshape(1, ATT_W), w_out.astype(BF16),
                 norm2_g.reshape(1, D))
    r2, e2, n1, c1 = _route(hn, w_q.astype(BF16), sub_keys.astype(BF16))
    return _peer(hn, peer_u.astype(BF16), peer_v.astype(BF16), r2, e2, n1, c1, h, final_norm_g.reshape(1, D))


def kernel(x, norm1_g, w_in, gla_gate_up_f, gla_gate_bias_f, gla_gate_up_b, gla_gate_bias_b, gla_norm_g,
           att_norm_g, w_out, norm2_g, peer_w_q, peer_sub_keys, peer_u, peer_v, final_norm_g):
    B, S, D = x.shape
    assert norm1_g.shape[0] == 1, "single trunk layer: the final norm is fused with the layer's last residual"
    out = _layer(x, norm1_g[0], w_in[0], gla_gate_up_f[0], gla_gate_bias_f[0], gla_gate_up_b[0],
                 gla_gate_bias_b[0], gla_norm_g[0], att_norm_g[0], w_out[0], norm2_g[0],
                 peer_w_q[0], peer_sub_keys[0], peer_u[0], peer_v[0], final_norm_g)
    return out.reshape(B, S, D)
```
